```python
import math, functools
import jax, jax.numpy as jnp
from jax import lax
import numpy as np

D_MODEL = 2048
BATCH = 1
SEQ = 8192
DEPTH = 2
DEC_BATCH = 32
DEC_SEQ = 4
PAST_LEN = 8192
PAGE_SIZE = 128

HEAD_DIM = 128
N_BRANCH = 4
BRANCH_W = D_MODEL // 4
CONV_DIM = BRANCH_W
CONV_W = 3
DSA_HEADS = BRANCH_W // HEAD_DIM
IDX_HEADS = 16
IDX_DIM = 64
TOPK_MAX = 256
FOX_HEADS = BRANCH_W // HEAD_DIM
HG_HEADS = 4
HG_DK = BRANCH_W // HG_HEADS
HG_DV = BRANCH_W // HG_HEADS
HG_W = HG_HEADS * HG_DK
HG_CHUNK = 64
Q_BLOCK = 128
D_FF = 4 * D_MODEL
ROPE_THETA = 500000.0
ROT_FRACTION = 4
EPS = 1e-6
NEG_BIG = -1e30
IDX_SCALE = (IDX_HEADS * IDX_DIM) ** -0.5
SECTION_WIDTHS = (CONV_DIM, CONV_DIM, CONV_DIM,
                  BRANCH_W, BRANCH_W, BRANCH_W,
                  IDX_HEADS * IDX_DIM, IDX_DIM, IDX_HEADS,
                  BRANCH_W, BRANCH_W, BRANCH_W, FOX_HEADS,
                  HG_W, HG_W, HG_W, HG_W,
                  N_BRANCH * D_MODEL)
N_IN = sum(SECTION_WIDTHS)

kernel_name = 'hybrid_gated_branch_decoder_step'

F32 = jnp.float32


def rms_norm(x, g):
    xf = x.astype(F32)
    y = xf * lax.rsqrt(jnp.mean(xf * xf, axis=-1, keepdims=True) + EPS)
    return (y * g.astype(F32)).astype(x.dtype)


def rope_partial(x, pos):
    d_rot = x.shape[-1] // ROT_FRACTION
    half = d_rot // 2
    inv = ROPE_THETA ** (-jnp.arange(half, dtype=F32) / half)
    ang = pos.astype(F32)[:, None] * inv[None, :]
    cos = jnp.cos(ang)[:, None, :]
    sin = jnp.sin(ang)[:, None, :]
    xr = x[..., :d_rot].astype(F32)
    x1, x2 = xr[..., :half], xr[..., half:]
    rot = jnp.concatenate([x1 * cos - x2 * sin, x2 * cos + x1 * sin], axis=-1).astype(x.dtype)
    return jnp.concatenate([rot, x[..., d_rot:]], axis=-1)


def take_rows(a, idx):
    return jax.vmap(lambda ab, ib: ab[ib])(a, idx)


def blockify(a, nb):
    return a.reshape((a.shape[0], nb, Q_BLOCK) + a.shape[2:]).swapaxes(0, 1)


def unblock(a):
    a = a.swapaxes(0, 1)
    return a.reshape(a.shape[0], a.shape[1] * a.shape[2], -1)


def project(xn, w_in_l, pos):
    B, L, _ = xn.shape
    z = jnp.einsum('bld,dn->bln', xn, w_in_l)
    splits = np.cumsum(SECTION_WIDTHS)[:-1].tolist()
    (c_h, c_b, c_c, a_q, a_k, a_v, i_q, i_k, i_w,
     f_q, f_k, f_v, f_f, h_q, h_f, h_i, h_g, gate) = jnp.split(z, splits, axis=-1)
    heads = lambda t, h: t.reshape(B, L, h, -1)
    dsa = (rope_partial(heads(a_q, DSA_HEADS), pos), rope_partial(heads(a_k, DSA_HEADS), pos),
           heads(a_v, DSA_HEADS), rope_partial(heads(i_q, IDX_HEADS), pos),
           rope_partial(i_k[:, :, None, :], pos)[:, :, 0], i_w * IDX_SCALE)
    fox = (heads(f_q, FOX_HEADS), heads(f_k, FOX_HEADS), heads(f_v, FOX_HEADS), f_f)
    return (c_h, c_b, c_c), dsa, fox, (h_q, h_f, h_i, h_g), gate.reshape(B, L, N_BRANCH, D_MODEL)


def short_conv(c_h, c_b, c_c, w, state):
    L = c_h.shape[1]
    u = c_c * c_h
    u_ext = jnp.concatenate([state.astype(u.dtype), u], axis=1)
    y = sum(w[j] * u_ext[:, j:j + L] for j in range(CONV_W))
    return c_b * y, u_ext[:, L:]


def index_topk(iq, iw, ik_all, q_pos, k_pos, n_sel):
    dots = jnp.einsum('bthd,bsd->bths', iq.astype(F32), ik_all.astype(F32))
    score = jnp.einsum('bth,bths->bts', iw.astype(F32), jax.nn.relu(dots))
    score = jnp.where(k_pos[None, None, :] <= q_pos[None, :, None], score, NEG_BIG)
    _, idx = lax.top_k(score, n_sel)
    return idx


def sparse_attend(q, k_sel, v_sel, valid):
    s = jnp.einsum('bthd,btkhd->bthk', q.astype(F32), k_sel.astype(F32)) * HEAD_DIM ** -0.5
    s = jnp.where(valid[:, :, None, :], s, NEG_BIG)
    p = jax.nn.softmax(s, axis=-1)
    return jnp.einsum('bthk,btkhd->bthd', p, v_sel.astype(F32)).astype(q.dtype)


def dsa_prompt(q, k, v, iq, ik, iw, pos):
    B, S = q.shape[:2]
    nb = S // Q_BLOCK
    n_sel = min(TOPK_MAX, S // 4)

    def one_block(args):
        qb, iqb, iwb, pb = args
        idx = index_topk(iqb, iwb, ik, pb, pos, n_sel)
        return sparse_attend(qb, take_rows(k, idx), take_rows(v, idx), idx <= pb[None, :, None])

    out = lax.map(one_block, (blockify(q, nb), blockify(iq, nb), blockify(iw, nb), pos.reshape(nb, Q_BLOCK)))
    return unblock(out)


def dsa_sample(q, k, v, iq, ik, iw, pos, layer, kpool, vpool, ipool, page_table):
    B, T = q.shape[:2]
    past = page_table.shape[1] * PAGE_SIZE
    ik_past = ipool[layer, page_table].reshape(B, past, IDX_DIM).astype(ik.dtype)
    ik_all = jnp.concatenate([ik_past, ik], axis=1)
    n_sel = min(TOPK_MAX, (past + T) // 4)
    idx = index_topk(iq, iw, ik_all, pos, jnp.arange(past + T), n_sel)
    is_past = (idx < past)[..., None, None]
    pidx = jnp.minimum(idx, past - 1)
    page = jnp.take_along_axis(page_table, (pidx // PAGE_SIZE).reshape(B, -1), axis=1).reshape(idx.shape)
    off = pidx % PAGE_SIZE
    nidx = jnp.clip(idx - past, 0, T - 1)
    k_sel = jnp.where(is_past, kpool[layer, page, off], take_rows(k, nidx))
    v_sel = jnp.where(is_past, vpool[layer, page, off], take_rows(v, nidx))
    out = sparse_attend(q, k_sel, v_sel, idx <= pos[None, :, None])
    return out.reshape(B, T, -1)


def fox_attend(q, c_q, k, v, c_k, q_pos, k_pos):
    s = jnp.einsum('bthd,bshd->bhts', q.astype(F32), k.astype(F32)) * HEAD_DIM ** -0.5
    s = s + jnp.swapaxes(c_q, 1, 2)[..., :, None] - jnp.swapaxes(c_k, 1, 2)[..., None, :]
    s = jnp.where(k_pos[None, :] <= q_pos[:, None], s, NEG_BIG)
    p = jax.nn.softmax(s, axis=-1)
    return jnp.einsum('bhts,bshd->bthd', p, v.astype(F32)).astype(q.dtype)


def fox_prompt(q, k, v, logf, pos):
    S = q.shape[1]
    nb = S // Q_BLOCK
    c = jnp.cumsum(logf, axis=1)

    def one_block(args):
        qb, cb, pb = args
        return fox_attend(qb, cb, k, v, c, pb, pos)

    return unblock(lax.map(one_block, (blockify(q, nb), blockify(c, nb), pos.reshape(nb, Q_BLOCK))))


def fox_sample(q, k, v, logf, pos, layer, kpool, vpool, lpool, page_table):
    B, T = q.shape[:2]
    past = page_table.shape[1] * PAGE_SIZE
    k_all = jnp.concatenate([kpool[layer, page_table].reshape(B, past, FOX_HEADS, HEAD_DIM).astype(k.dtype), k], axis=1)
    v_all = jnp.concatenate([vpool[layer, page_table].reshape(B, past, FOX_HEADS, HEAD_DIM).astype(v.dtype), v], axis=1)
    lf_all = jnp.concatenate([lpool[layer, page_table].reshape(B, past, FOX_HEADS).astype(F32), logf], axis=1)
    c = jnp.cumsum(lf_all, axis=1)
    out = fox_attend(q, c[:, past:], k_all, v_all, c, pos, jnp.arange(past + T))
    return out.reshape(B, T, -1)


def hgrn_chunked(q, k, i, lf, S0):
    B, L, H, _ = q.shape
    c = math.gcd(L, HG_CHUNK)
    n = L // c
    chunks = lambda a: a.reshape((B, n, c) + a.shape[2:]).swapaxes(0, 1)
    causal = jnp.tril(jnp.ones((c, c), dtype=bool))[None, :, :, None, None]

    def step(S, xs):
        qc, kc, ic, lc = xs
        b = jnp.cumsum(lc, axis=1)
        o_inter = jnp.einsum('bthk,bhkv->bthv', qc * jnp.exp(b), S)
        decay = jnp.exp(jnp.where(causal, b[:, :, None] - b[:, None, :], NEG_BIG))
        A = jnp.einsum('bthk,btshk,bshk->bhts', qc, decay, kc)
        o_intra = jnp.einsum('bhts,bshv->bthv', A, ic)
        b_last = b[:, -1]
        S = jnp.exp(b_last)[..., None] * S + jnp.einsum('bshk,bshv->bhkv', kc * jnp.exp(b_last[:, None] - b), ic)
        return S, o_inter + o_intra

    S, o = lax.scan(step, S0, (chunks(q), chunks(k), chunks(i), chunks(lf)))
    return o.swapaxes(0, 1).reshape(B, L, H, -1), S


def hgrn_branch(h_q, h_f, h_i, h_g, lb, norm_g, S0):
    B, L, _ = h_q.shape
    z = h_f.astype(F32)
    logf = jnp.log(lb + (1.0 - lb) * jax.nn.sigmoid(z))
    k = (1.0 - lb) * jax.nn.sigmoid(-z)
    hd = lambda t, d: t.reshape(B, L, HG_HEADS, d)
    o, S = hgrn_chunked(hd(h_q.astype(F32), HG_DK), hd(k, HG_DK), hd(h_i.astype(F32), HG_DV),
                        hd(logf, HG_DK), S0.astype(F32))
    o = o * lax.rsqrt(jnp.mean(o * o, axis=-1, keepdims=True) + EPS) * norm_g.astype(F32).reshape(HG_HEADS, HG_DV)
    o = o.reshape(B, L, HG_W) * jax.nn.silu(h_g.astype(F32))
    return o.astype(h_q.dtype), S


def token_mixer(xn, pos, conv_state, hgrn_state, lb, dsa_fn, fox_fn,
                w_in_l, conv_w_l, fox_fb_l, hg_norm_l, w_br_l, w_out_l):
    conv_in, dsa_in, fox_in, hg_in, gate = project(xn, w_in_l, pos)
    y_conv, conv_new = short_conv(*conv_in, conv_w_l, conv_state)
    a_q, a_k, a_v, i_q, i_k, i_w = dsa_in
    y_dsa = dsa_fn(a_q, a_k, a_v, i_q, i_k, i_w, pos)
    f_q, f_k, f_v, f_f = fox_in
    logf = jax.nn.log_sigmoid(f_f.astype(F32) + fox_fb_l.astype(F32))
    y_fox = fox_fn(f_q, f_k, f_v, logf, pos)
    y_hg, S_new = hgrn_branch(*hg_in, lb, hg_norm_l, hgrn_state)
    branches = jnp.stack([y_conv, y_dsa, y_fox, y_hg], axis=2)
    proj = jnp.einsum('blnc,ncd->blnd', branches, w_br_l)
    g = jax.nn.sigmoid(gate.astype(F32)).astype(proj.dtype)
    y = jnp.einsum('blnd,de->ble', g * proj, w_out_l)
    return y, (a_k, a_v, i_k, f_k, f_v, logf, conv_new, S_new)


def ffn(x, g, w_up_l, w_down_l):
    h = jnp.einsum('bld,df->blf', rms_norm(x, g), w_up_l)
    return jnp.einsum('blf,fd->bld', jnp.square(jax.nn.relu(h)), w_down_l)


def setup_inputs(seed: int = 0) -> dict:
    key = jax.random.key(seed)
    ks = jax.random.split(key, 24)
    n_pages = PAST_LEN // PAGE_SIZE
    n_used = DEC_BATCH * n_pages
    n_pool = n_used + max(1, n_used // 4)
    nrm = lambda k, shape, s=1.0: jax.random.normal(k, shape, F32) * s
    page_table = jax.random.permutation(ks[0], n_pool)[:n_used].reshape(DEC_BATCH, n_pages).astype(jnp.int32)
    kv_shape = (DEPTH, n_pool, PAGE_SIZE, DSA_HEADS, HEAD_DIM)
    fox_shape = (DEPTH, n_pool, PAGE_SIZE, FOX_HEADS, HEAD_DIM)
    return dict(
        x_prompt=nrm(ks[1], (BATCH, SEQ, D_MODEL)),
        x_sample=nrm(ks[2], (DEC_BATCH, DEC_SEQ, D_MODEL)),
        cache_dsa_k=nrm(ks[3], kv_shape),
        cache_dsa_v=nrm(ks[4], kv_shape),
        cache_dsa_kidx=nrm(ks[5], (DEPTH, n_pool, PAGE_SIZE, IDX_DIM)),
        cache_fox_k=nrm(ks[6], fox_shape),
        cache_fox_v=nrm(ks[7], fox_shape),
        cache_fox_logf=jax.nn.log_sigmoid(nrm(ks[8], (DEPTH, n_pool, PAGE_SIZE, FOX_HEADS))),
        state_conv=nrm(ks[9], (DEPTH, DEC_BATCH, CONV_W - 1, CONV_DIM)),
        state_hgrn=nrm(ks[10], (DEPTH, DEC_BATCH, HG_HEADS, HG_DK, HG_DV), 0.5),
        page_table=page_table,
        norm_mix_g=1.0 + nrm(ks[11], (DEPTH, D_MODEL), 0.1),
        w_in=nrm(ks[12], (DEPTH, D_MODEL, N_IN), D_MODEL ** -0.5),
        conv_w=nrm(ks[13], (DEPTH, CONV_W, CONV_DIM), CONV_W ** -0.5),
        fox_fb=nrm(ks[14], (DEPTH, FOX_HEADS), 0.1),
        hgrn_lb_logits=nrm(ks[15], (DEPTH, HG_W), 0.1),
        hgrn_norm_g=1.0 + nrm(ks[16], (DEPTH, HG_W), 0.1),
        w_branch=nrm(ks[17], (DEPTH, N_BRANCH, BRANCH_W, D_MODEL), BRANCH_W ** -0.5),
        w_out=nrm(ks[18], (DEPTH, D_MODEL, D_MODEL), D_MODEL ** -0.5),
        norm_ffn_g=1.0 + nrm(ks[19], (DEPTH, D_MODEL), 0.1),
        w_up=nrm(ks[20], (DEPTH, D_MODEL, D_FF), D_MODEL ** -0.5),
        w_down=nrm(ks[21], (DEPTH, D_FF, D_MODEL), D_FF ** -0.5),
        final_norm_g=1.0 + nrm(ks[22], (D_MODEL,), 0.1),
    )


def reference(x_prompt, x_sample, cache_dsa_k, cache_dsa_v, cache_dsa_kidx, cache_fox_k, cache_fox_v,
              cache_fox_logf, state_conv, state_hgrn, page_table, norm_mix_g, w_in, conv_w, fox_fb,
              hgrn_lb_logits, hgrn_norm_g, w_branch, w_out, norm_ffn_g, w_up, w_down, final_norm_g):
    p_lb = jax.nn.softmax(hgrn_lb_logits.astype(F32), axis=0)
    lower_bounds = jnp.cumsum(p_lb, axis=0) - p_lb[0]
    Bp, Lp = x_prompt.shape[:2]
    pos_p = jnp.arange(Lp)
    pos_s = PAST_LEN + jnp.arange(x_sample.shape[1])
    conv0 = jnp.zeros((Bp, CONV_W - 1, CONV_DIM), x_prompt.dtype)
    hg0 = jnp.zeros((Bp, HG_HEADS, HG_DK, HG_DV), F32)
    hp, hs = x_prompt, x_sample
    new_p, new_s = [], []
    for l in range(DEPTH):
        lw = (w_in[l], conv_w[l], fox_fb[l], hgrn_norm_g[l], w_branch[l], w_out[l])
        dp, st_p = token_mixer(rms_norm(hp, norm_mix_g[l]), pos_p, conv0, hg0, lower_bounds[l],
                               dsa_prompt, fox_prompt, *lw)
        hp = hp + dp
        hp = hp + ffn(hp, norm_ffn_g[l], w_up[l], w_down[l])
        new_p.append(st_p)
        dsa_fn = functools.partial(dsa_sample, layer=l, kpool=cache_dsa_k, vpool=cache_dsa_v,
                                   ipool=cache_dsa_kidx, page_table=page_table)
        fox_fn = functools.partial(fox_sample, layer=l, kpool=cache_fox_k, vpool=cache_fox_v,
                                   lpool=cache_fox_logf, page_table=page_table)
        ds, st_s = token_mixer(rms_norm(hs, norm_mix_g[l]), pos_s, state_conv[l], state_hgrn[l],
                               lower_bounds[l], dsa_fn, fox_fn, *lw)
        hs = hs + ds
        hs = hs + ffn(hs, norm_ffn_g[l], w_up[l], w_down[l])
        new_s.append(st_s)
    y_prompt = rms_norm(hp, final_norm_g)
    y_sample = rms_norm(hs, final_norm_g)
    (p_dsa_k, p_dsa_v, p_dsa_kidx, p_fox_k, p_fox_v, p_fox_logf, p_conv, p_hgrn) = [jnp.stack(t) for t in zip(*new_p)]
    (s_dsa_k, s_dsa_v, s_dsa_kidx, s_fox_k, s_fox_v, s_fox_logf, s_conv, s_hgrn) = [jnp.stack(t) for t in zip(*new_s)]
    return (y_prompt, y_sample,
            p_dsa_k, p_dsa_v, p_dsa_kidx, p_fox_k, p_fox_v, p_fox_logf, p_conv, p_hgrn,
            s_dsa_k, s_dsa_v, s_dsa_kidx, s_fox_k, s_fox_v, s_fox_logf, s_conv, s_hgrn)
```

```python
import functools

import jax
import jax.numpy as jnp
from jax import lax
from jax.experimental import pallas as pl
from jax.experimental.pallas import tpu as pltpu

F32 = jnp.float32
BF16 = jnp.bfloat16
I32 = jnp.int32

LANES = 128
HEAD_DIM = 128
IDX_DIM = 64
IDX_HEADS = 16
CONV_W = 3
TOPK_MAX = 256
ROPE_THETA = 500000.0
ROT_FRACTION = 4
EPS = 1e-6
NEG_BIG = -1e30
INT_MIN = -2 ** 31
SAMPLE_ROWS = 8
HG_CHUNK_PROMPT = 16
VMEM_LIMIT = 56 * 1024 * 1024
HIGHEST = lax.Precision.HIGHEST


def _dot(a, b, precision=None):
    return jnp.dot(a, b, preferred_element_type=F32, precision=precision)


def _dot_nt(a, b):
    return lax.dot_general(a, b, (((1,), (1,)), ((), ())), preferred_element_type=F32)


def _dot_tn(a, b):
    return lax.dot_general(a, b, (((0,), (0,)), ((), ())), preferred_element_type=F32)


def _pick(n, cands):
    for c in cands:
        if n % c == 0:
            return c
    raise ValueError(f"no tile for {n} in {cands}")


def _params(sem, vmem=VMEM_LIMIT):
    return pltpu.CompilerParams(dimension_semantics=sem, vmem_limit_bytes=vmem)


def _log_sigmoid(x):
    return jnp.minimum(x, 0.0) - jnp.log1p(jnp.exp(-jnp.abs(x)))


def _rms(x, g):
    return x * lax.rsqrt(jnp.mean(x * x, axis=-1, keepdims=True) + EPS) * g


class _Layout:
    def __init__(self, d_model):
        bw = d_model // 4
        assert bw % HEAD_DIM == 0
        self.d = d_model
        self.bw = bw
        self.nh = bw // HEAD_DIM
        self.iqw = IDX_HEADS * LANES
        assert self.iqw % bw == 0 and IDX_HEADS + self.nh <= LANES
        o = 0
        for name in ("ch", "cb", "cc", "aq", "ak", "av", "fq", "fk"):
            setattr(self, name, o)
            o += bw
        self.iq = o
        o += self.iqw
        for name in ("fv", "hq", "hf", "hi", "hg"):
            setattr(self, name, o)
            o += bw
        self.ikp = o
        o += LANES
        self.misc = o
        o += LANES
        o = -(-o // bw) * bw
        self.gate = o
        o += 4 * d_model
        self.n = o
        self.ff_lane = IDX_HEADS


def _relayout_w_in(w_in, lay):
    bw, nh = lay.bw, lay.nh
    dep, d, _ = w_in.shape
    o = 0

    def take(width):
        nonlocal o
        s = w_in[:, :, o:o + width]
        o += width
        return s

    first6 = take(6 * bw)
    iq = take(IDX_HEADS * IDX_DIM)
    ik = take(IDX_DIM)
    iw = take(IDX_HEADS)
    fq, fk, fv = take(bw), take(bw), take(bw)
    ff = take(nh)
    h4 = take(4 * bw)
    gate = take(4 * lay.d)
    assert o == w_in.shape[-1]
    iq_pad = jnp.pad(iq.reshape(dep, d, IDX_HEADS, IDX_DIM),
                     ((0, 0), (0, 0), (0, 0), (0, LANES - IDX_DIM))).reshape(dep, d, lay.iqw)
    ik_pad = jnp.pad(ik, ((0, 0), (0, 0), (0, LANES - IDX_DIM)))
    misc = jnp.pad(jnp.concatenate([iw, ff], axis=-1), ((0, 0), (0, 0), (0, LANES - IDX_HEADS - nh)))
    pad = jnp.zeros((dep, d, lay.gate - lay.misc - LANES), w_in.dtype)
    out = jnp.concatenate([first6, fq, fk, iq_pad, fv, h4, ik_pad, misc, pad, gate], axis=-1)
    assert out.shape[-1] == lay.n
    return out.astype(BF16)


def _rope_tables(pos, head_dim):
    d_rot = head_dim // ROT_FRACTION
    half = d_rot // 2
    inv = ROPE_THETA ** (-jnp.arange(half, dtype=F32) / half)
    ang = pos.astype(F32)[:, None] * inv[None, :]
    cos, sin = jnp.cos(ang), jnp.sin(ang)
    t = pos.shape[0]
    rest = LANES - d_rot
    c = jnp.concatenate([cos, cos, jnp.ones((t, rest), F32)], axis=1)
    s1 = jnp.concatenate([-sin, jnp.zeros((t, half + rest), F32)], axis=1)
    s2 = jnp.concatenate([jnp.zeros((t, half), F32), sin, jnp.zeros((t, rest), F32)], axis=1)
    return jnp.concatenate([c, s1, s2], axis=1), half


def _in_proj_kernel(x_ref, g_ref, w_ref, z_ref, xn_ref):
    @pl.when(pl.program_id(1) == 0)
    def _():
        xn_ref[...] = _rms(x_ref[...], g_ref[...]).astype(BF16)

    z_ref[...] = _dot(xn_ref[...], w_ref[...])


def _in_proj(x, g, w_r, layer):
    t, d = x.shape
    n = w_r.shape[-1]
    tm = _pick(t, (640, 512, 256, 128))
    tn = _pick(n, (1024, 512))
    return pl.pallas_call(
        _in_proj_kernel,
        grid=(t // tm, n // tn),
        in_specs=[pl.BlockSpec((tm, d), lambda i, j: (i, 0)),
                  pl.BlockSpec((None, 1, d), lambda i, j: (layer, 0, 0)),
                  pl.BlockSpec((None, d, tn), lambda i, j: (layer, 0, j))],
        out_specs=pl.BlockSpec((tm, tn), lambda i, j: (i, j)),
        out_shape=jax.ShapeDtypeStruct((t, n), F32),
        scratch_shapes=[pltpu.VMEM((tm, d), BF16)],
        compiler_params=_params(("parallel", "arbitrary")),
        name="in_proj",
    )(x, g, w_r)


def _rope(x, tab, half, reps):
    w = x.shape[1]
    c = jnp.concatenate([tab[:, 0:LANES]] * reps, axis=1)
    s1 = jnp.concatenate([tab[:, LANES:2 * LANES]] * reps, axis=1)
    s2 = jnp.concatenate([tab[:, 2 * LANES:3 * LANES]] * reps, axis=1)
    return x * c + pltpu.roll(x, w - half, 1) * s1 + pltpu.roll(x, half, 1) * s2


def _post_kernel(aq_ref, ak_ref, av_ref, iq_ref, ikp_ref, misc_ref, ta_ref, ti_ref, fb_ref,
                 aqr_ref, akr_ref, akb_ref, avb_ref, iqr_ref, ikr_ref, ikb_ref, m2_ref,
                 *, nh, half_a, half_i, idx_scale):
    ta, ti = ta_ref[...], ti_ref[...]
    aqr_ref[...] = _rope(aq_ref[...], ta, half_a, nh).astype(BF16)
    akr = _rope(ak_ref[...], ta, half_a, nh)
    akr_ref[...] = akr
    akb_ref[...] = akr.astype(BF16)
    avb_ref[...] = av_ref[...].astype(BF16)
    iqr_ref[...] = _rope(iq_ref[...], ti, half_i, IDX_HEADS).astype(BF16)
    ikr = _rope(ikp_ref[...], ti, half_i, 1)
    ikr_ref[...] = ikr
    ikb_ref[...] = ikr.astype(BF16)
    m = misc_ref[...]
    lane = lax.broadcasted_iota(I32, m.shape, 1)
    logf = _log_sigmoid(m + fb_ref[...])
    m2_ref[...] = jnp.where(lane < IDX_HEADS, m * idx_scale,
                            jnp.where(lane < IDX_HEADS + nh, logf, 0.0))


def _post(z, tab_a, tab_i, fb_row, lay, layer, half_a, half_i):
    t = z.shape[0]
    bw = lay.bw
    tm = _pick(t, (320, 256, 128, 64, 32, 16))
    col = lambda off, w: pl.BlockSpec((tm, w), lambda i: (i, off // w))
    row = lambda w: pl.BlockSpec((tm, w), lambda i: (i, 0))
    kern = functools.partial(_post_kernel, nh=lay.nh, half_a=half_a, half_i=half_i,
                             idx_scale=float((IDX_HEADS * IDX_DIM) ** -0.5))
    return pl.pallas_call(
        kern,
        grid=(t // tm,),
        in_specs=[col(lay.aq, bw), col(lay.ak, bw), col(lay.av, bw), col(lay.iq, lay.iqw),
                  col(lay.ikp, LANES), col(lay.misc, LANES), row(3 * LANES), row(3 * LANES),
                  pl.BlockSpec((None, 1, LANES), lambda i: (layer, 0, 0))],
        out_specs=[row(bw), row(bw), row(bw), row(bw), row(lay.iqw), row(LANES), row(LANES), row(LANES)],
        out_shape=[jax.ShapeDtypeStruct((t, bw), BF16), jax.ShapeDtypeStruct((t, bw), F32),
                   jax.ShapeDtypeStruct((t, bw), BF16), jax.ShapeDtypeStruct((t, bw), BF16),
                   jax.ShapeDtypeStruct((t, lay.iqw), BF16), jax.ShapeDtypeStruct((t, LANES), F32),
                   jax.ShapeDtypeStruct((t, LANES), BF16), jax.ShapeDtypeStruct((t, LANES), F32)],
        compiler_params=_params(("parallel",)),
        name="post_proj",
    )(z, z, z, z, z, z, tab_a, tab_i, fb_row)


def _mix_kernel(b0_ref, b1_ref, b2_ref, b3_ref, g0_ref, g1_ref, g2_ref, g3_ref,
                wbr_ref, wout_ref, x_ref, o_ref, acc_ref):
    dt = pl.program_id(1)

    @pl.when(dt == 0)
    def _():
        acc_ref[...] = jnp.zeros_like(acc_ref)

    m = None
    for n, (b_ref, g_ref) in enumerate(((b0_ref, g0_ref), (b1_ref, g1_ref), (b2_ref, g2_ref), (b3_ref, g3_ref))):
        term = jax.nn.sigmoid(g_ref[...]) * _dot(b_ref[...], wbr_ref[n])
        m = term if m is None else m + term
    acc_ref[...] += _dot(m.astype(BF16), wout_ref[...])

    @pl.when(dt == pl.num_programs(1) - 1)
    def _():
        o_ref[...] = x_ref[...] + acc_ref[...]


def _mix(branches, z, w_br, w_out, x, lay, layer):
    t, d = x.shape
    bw = lay.bw
    tm = _pick(t, (640, 512, 256, 128))
    td = 512
    assert d % td == 0 and lay.gate % td == 0
    gspec = lambda n: pl.BlockSpec((tm, td), lambda i, j: (i, (lay.gate + n * d) // td + j))
    bspec = pl.BlockSpec((tm, bw), lambda i, j: (i, 0))
    return pl.pallas_call(
        _mix_kernel,
        grid=(t // tm, d // td),
        in_specs=[bspec, bspec, bspec, bspec, gspec(0), gspec(1), gspec(2), gspec(3),
                  pl.BlockSpec((None, 4, bw, td), lambda i, j: (layer, 0, 0, j)),
                  pl.BlockSpec((None, td, d), lambda i, j: (layer, j, 0)),
                  pl.BlockSpec((tm, d), lambda i, j: (i, 0))],
        out_specs=pl.BlockSpec((tm, d), lambda i, j: (i, 0)),
        out_shape=jax.ShapeDtypeStruct((t, d), F32),
        scratch_shapes=[pltpu.VMEM((tm, d), F32)],
        compiler_params=_params(("parallel", "arbitrary")),
        name="branch_mix",
    )(*branches, z, z, z, z, w_br, w_out, x)


def _ffn_kernel(x_ref, g_ref, wu_ref, wd_ref, o_ref, xn_ref, acc_ref):
    f = pl.program_id(1)

    @pl.when(f == 0)
    def _():
        xn_ref[...] = _rms(x_ref[...], g_ref[...]).astype(BF16)
        acc_ref[...] = jnp.zeros_like(acc_ref)

    h = jnp.maximum(_dot(xn_ref[...], wu_ref[...]), 0.0)
    acc_ref[...] += _dot((h * h).astype(BF16), wd_ref[...])

    @pl.when(f == pl.num_programs(1) - 1)
    def _():
        o_ref[...] = x_ref[...] + acc_ref[...]


def _ffn(x, g, w_up, w_down, layer):
    t, d = x.shape
    dff = w_up.shape[-1]
    tm = _pick(t, (640, 512, 256, 128))
    tf = _pick(dff, (512, 256, 128))
    return pl.pallas_call(
        _ffn_kernel,
        grid=(t // tm, dff // tf),
        in_specs=[pl.BlockSpec((tm, d), lambda i, j: (i, 0)),
                  pl.BlockSpec((None, 1, d), lambda i, j: (layer, 0, 0)),
                  pl.BlockSpec((None, d, tf), lambda i, j: (layer, 0, j)),
                  pl.BlockSpec((None, tf, d), lambda i, j: (layer, j, 0))],
        out_specs=pl.BlockSpec((tm, d), lambda i, j: (i, 0)),
        out_shape=jax.ShapeDtypeStruct((t, d), F32),
        scratch_shapes=[pltpu.VMEM((tm, d), BF16), pltpu.VMEM((tm, d), F32)],
        compiler_params=_params(("parallel", "arbitrary")),
        name="ffn",
    )(x, g, w_up, w_down)


def _final_norm_kernel(x_ref, g_ref, o_ref):
    o_ref[...] = _rms(x_ref[...], g_ref[...])


def _final_norm(x, g):
    t, d = x.shape
    tm = _pick(t, (640, 512, 256, 128))
    return pl.pallas_call(
        _final_norm_kernel,
        grid=(t // tm,),
        in_specs=[pl.BlockSpec((tm, d), lambda i: (i, 0)), pl.BlockSpec((1, d), lambda i: (0, 0))],
        out_specs=pl.BlockSpec((tm, d), lambda i: (i, 0)),
        out_shape=jax.ShapeDtypeStruct((t, d), F32),
        compiler_params=_params(("parallel",)),
        name="final_norm",
    )(x, g)


def _conv_prompt_kernel(ch_ref, cb_ref, cc_ref, w_ref, y_ref, last_ref, carry_ref):
    @pl.when(pl.program_id(0) == 0)
    def _():
        carry_ref[...] = jnp.zeros_like(carry_ref)

    u = cc_ref[...] * ch_ref[...]
    tm = u.shape[0]
    rid = lax.broadcasted_iota(I32, u.shape, 0)
    prev = carry_ref[...]
    acc = w_ref[CONV_W - 1:CONV_W, :] * u
    for back in range(1, CONV_W):
        head = jnp.concatenate([pltpu.roll(prev, back, 0), jnp.zeros((tm - 8, u.shape[1]), F32)], axis=0)
        shifted = jnp.where(rid < back, head, pltpu.roll(u, back, 0))
        acc = acc + w_ref[CONV_W - 1 - back:CONV_W - back, :] * shifted
    y_ref[...] = (cb_ref[...] * acc).astype(y_ref.dtype)
    carry_ref[...] = u[tm - 8:tm, :]
    last_ref[...] = u[tm - 8:tm, :]


def _conv_prompt(z, conv_w, tp, lay, layer):
    bw = lay.bw
    tm = _pick(tp, (512, 256, 128))
    col = lambda off: pl.BlockSpec((tm, bw), lambda i: (i, off // bw))
    return pl.pallas_call(
        _conv_prompt_kernel,
        grid=(tp // tm,),
        in_specs=[col(lay.ch), col(lay.cb), col(lay.cc),
                  pl.BlockSpec((None, CONV_W, bw), lambda i: (layer, 0, 0))],
        out_specs=[pl.BlockSpec((tm, bw), lambda i: (i, 0)), pl.BlockSpec((8, bw), lambda i: (0, 0))],
        out_shape=[jax.ShapeDtypeStruct((tp, bw), BF16), jax.ShapeDtypeStruct((8, bw), F32)],
        scratch_shapes=[pltpu.VMEM((8, bw), F32)],
        compiler_params=_params(("arbitrary",)),
        name="conv_prompt",
    )(z, z, z, conv_w)


def _conv_sample_kernel(ch_ref, cb_ref, cc_ref, st_ref, w_ref, y_ref, ns_ref, *, ls, bw):
    u_ext = jnp.concatenate([st_ref[...], cc_ref[...] * ch_ref[...]], axis=1)
    cb = cb_ref[...]
    ys = []
    for t in range(ls):
        acc = None
        for j in range(CONV_W):
            term = w_ref[j:j + 1, :] * u_ext[:, (t + j) * bw:(t + j + 1) * bw]
            acc = term if acc is None else acc + term
        ys.append(cb[:, t * bw:(t + 1) * bw] * acc)
    y_ref[...] = jnp.concatenate(ys, axis=1).astype(y_ref.dtype)
    ns_ref[...] = u_ext[:, ls * bw:(ls + CONV_W - 1) * bw]


def _conv_sample(ch, cb, cc, state, conv_w, layer, bs, ls, bw):
    kern = functools.partial(_conv_sample_kernel, ls=ls, bw=bw)
    full = lambda shape: pl.BlockSpec(shape, lambda i: (0,) * len(shape))
    return pl.pallas_call(
        kern,
        grid=(1,),
        in_specs=[full((bs, ls * bw)), full((bs, ls * bw)), full((bs, ls * bw)),
                  pl.BlockSpec((None, bs, (CONV_W - 1) * bw), lambda i: (layer, 0, 0)),
                  pl.BlockSpec((None, CONV_W, bw), lambda i: (layer, 0, 0))],
        out_specs=[full((bs, ls * bw)), full((bs, (CONV_W - 1) * bw))],
        out_shape=[jax.ShapeDtypeStruct((bs, ls * bw), BF16),
                   jax.ShapeDtypeStruct((bs, (CONV_W - 1) * bw), F32)],
        compiler_params=_params(("arbitrary",)),
        name="conv_sample",
    )(ch, cb, cc, state, conv_w)


def _f32_key(x):
    bits = lax.bitcast_convert_type(x, I32)
    return jnp.where(bits < 0, bits ^ jnp.int32(0x7FFFFFFF), bits)


def _kth_largest_key(get_chunk, nch, k, rows, width):
    def count_ge(cand):
        def body(c, part):
            m = jnp.where(get_chunk(c) >= cand, 1.0, 0.0)
            for b in range(width // LANES):
                part = part + m[:, b * LANES:(b + 1) * LANES]
            return part

        part = lax.fori_loop(0, nch, body, jnp.zeros((rows, LANES), F32))
        return jnp.sum(part, axis=1, keepdims=True)

    kf = float(k)
    base = jnp.where(count_ge(jnp.zeros((rows, 1), I32)) >= kf, jnp.int32(0), jnp.int32(INT_MIN))

    def bit_body(it, base):
        cand = base | jnp.left_shift(jnp.int32(1), 30 - it)
        return jnp.where(count_ge(cand) >= kf, cand, base)

    return lax.fori_loop(0, 31, bit_body, base)


def _dsa_prompt_kernel(iq_ref, mw_ref, aq_ref, ik_ref, ak_ref, av_ref, o_ref, sc_ref,
                       *, tq, tk, n_sel, nh, scale):
    i = pl.program_id(0)
    nch = ((i + 1) * tq + tk - 1) // tk
    row = i * tq + lax.broadcasted_iota(I32, (tq, tk), 0)
    col0 = lax.broadcasted_iota(I32, (tq, tk), 1)
    w = mw_ref[...]

    def score_body(c, carry):
        off = pl.multiple_of(c * tk, tk)
        ikc = ik_ref[pl.ds(off, tk), :]
        acc = jnp.zeros((tq, tk), F32)
        for h in range(IDX_HEADS):
            d = _dot_nt(iq_ref[:, h * LANES:(h + 1) * LANES], ikc)
            acc = acc + w[:, h:h + 1] * jnp.maximum(d, 0.0)
        s = jnp.where(off + col0 <= row, acc + 0.0, NEG_BIG)
        sc_ref[c] = _f32_key(s)
        return carry

    lax.fori_loop(0, nch, score_body, 0)
    tau = _kth_largest_key(lambda c: sc_ref[c], nch, n_sel, tq, tk)

    outs = []
    for h in range(nh):
        qh = aq_ref[:, h * HEAD_DIM:(h + 1) * HEAD_DIM]

        def att_body(c, carry, h=h, qh=qh):
            m, l, acc = carry
            off = pl.multiple_of(c * tk, tk)
            kc = ak_ref[pl.ds(off, tk), h * HEAD_DIM:(h + 1) * HEAD_DIM]
            vc = av_ref[pl.ds(off, tk), h * HEAD_DIM:(h + 1) * HEAD_DIM]
            msk = (sc_ref[c] >= tau) & (off + col0 <= row)
            s = jnp.where(msk, _dot_nt(qh, kc) * scale, NEG_BIG)
            m_new = jnp.maximum(m, jnp.max(s, axis=1, keepdims=True))
            p = jnp.where(msk, jnp.exp(s - m_new), 0.0)
            alpha = jnp.exp(m - m_new)
            l = l * alpha + jnp.sum(p, axis=1, keepdims=True)
            acc = acc * alpha + _dot(p.astype(BF16), vc)
            return m_new, l, acc

        init = (jnp.full((tq, 1), NEG_BIG, F32), jnp.zeros((tq, 1), F32), jnp.zeros((tq, HEAD_DIM), F32))
        _, l, acc = lax.fori_loop(0, nch, att_body, init)
        outs.append(acc / l)
    o_ref[...] = jnp.concatenate(outs, axis=1).astype(o_ref.dtype)


def _dsa_prompt(iq_r, misc2, aq_r, ik_b, ak_b, av_b, tp, lay):
    bw = lay.bw
    tq = 128
    tk = _pick(tp, (512, 256, 128))
    n_sel = min(TOPK_MAX, tp // 4)
    kern = functools.partial(_dsa_prompt_kernel, tq=tq, tk=tk, n_sel=n_sel, nh=lay.nh,
                             scale=float(HEAD_DIM ** -0.5))
    res = lambda w: pl.BlockSpec((tp, w), lambda i: (0, 0))
    return pl.pallas_call(
        kern,
        grid=(tp // tq,),
        in_specs=[pl.BlockSpec((tq, lay.iqw), lambda i: (i, 0)),
                  pl.BlockSpec((tq, LANES), lambda i: (i, 0)),
                  pl.BlockSpec((tq, bw), lambda i: (i, 0)),
                  res(LANES), res(bw), res(bw)],
        out_specs=pl.BlockSpec((tq, bw), lambda i: (i, 0)),
        out_shape=jax.ShapeDtypeStruct((tp, bw), BF16),
        scratch_shapes=[pltpu.VMEM((tp // tk, tq, tk), I32)],
        compiler_params=_params(("arbitrary",)),
        name="dsa_prompt",
    )(iq_r, misc2, aq_r, ik_b, ak_b, av_b)


def _new_key_valid(shape, ls):
    t = lax.broadcasted_iota(I32, shape, 0) & (SAMPLE_ROWS - 1)
    c = lax.broadcasted_iota(I32, shape, 1)
    return (c <= t) & (c < ls)


def _dsa_sample_score_kernel(pt_ref, iq_ref, iw_ref, pool_ref, new_ref, key_ref, tau_ref, sc_ref,
                             *, npg, page, ls, n_sel):
    j = pl.program_id(1)
    is_new = j == npg
    keys = jnp.where(is_new, new_ref[...], pool_ref[...]).astype(BF16)
    d = _dot_nt(iq_ref[:, 0:IDX_DIM], keys)
    r = jnp.maximum(d, 0.0) * iw_ref[:, 0:1]
    s = r[0:SAMPLE_ROWS, :]
    for h in range(1, IDX_HEADS):
        s = s + r[h * SAMPLE_ROWS:(h + 1) * SAMPLE_ROWS, :]
    valid = jnp.logical_or(jnp.logical_not(is_new), _new_key_valid(s.shape, ls))
    key = _f32_key(jnp.where(valid, s + 0.0, NEG_BIG))
    key_ref[...] = key
    sc_ref[j] = key

    @pl.when(is_new)
    def _():
        tau = _kth_largest_key(lambda c: sc_ref[c], npg + 1, n_sel, SAMPLE_ROWS, page)
        tau_ref[...] = jnp.broadcast_to(tau, tau_ref.shape)


def _dsa_sample_score(page_table, iq_s, iw_s, pool, ik_new, layer, ls):
    bs, npg = page_table.shape
    page = pool.shape[2]
    n_sel = min(TOPK_MAX, (npg * page + ls) // 4)
    kern = functools.partial(_dsa_sample_score_kernel, npg=npg, page=page, ls=ls, n_sel=n_sel)
    grid_spec = pltpu.PrefetchScalarGridSpec(
        num_scalar_prefetch=1,
        grid=(bs, npg + 1),
        in_specs=[pl.BlockSpec((None, IDX_HEADS * SAMPLE_ROWS, LANES), lambda b, j, pt: (b, 0, 0)),
                  pl.BlockSpec((None, IDX_HEADS * SAMPLE_ROWS, LANES), lambda b, j, pt: (b, 0, 0)),
                  pl.BlockSpec((None, None, page, IDX_DIM),
                               lambda b, j, pt: (layer, pt[b, jnp.minimum(j, npg - 1)], 0, 0)),
                  pl.BlockSpec((None, page, IDX_DIM), lambda b, j, pt: (b, 0, 0))],
        out_specs=[pl.BlockSpec((None, SAMPLE_ROWS, page), lambda b, j, pt: (b, 0, j)),
                   pl.BlockSpec((None, SAMPLE_ROWS, LANES), lambda b, j, pt: (b, 0, 0))],
        scratch_shapes=[pltpu.VMEM((npg + 1, SAMPLE_ROWS, page), I32)],
    )
    return pl.pallas_call(
        kern,
        grid_spec=grid_spec,
        out_shape=[jax.ShapeDtypeStruct((bs, SAMPLE_ROWS, (npg + 1) * page), I32),
                   jax.ShapeDtypeStruct((bs, SAMPLE_ROWS, LANES), I32)],
        compiler_params=_params(("parallel", "arbitrary")),
        name="dsa_sample_score",
    )(page_table, iq_s, iw_s, pool, ik_new)


def _block_diag_q(q, nh):
    qq = jnp.concatenate([q.astype(BF16)] * nh, axis=0)
    rh = lax.broadcasted_iota(I32, qq.shape, 0) >> 3
    lh = lax.broadcasted_iota(I32, qq.shape, 1) >> 7
    return jnp.where(rh == lh, qq, jnp.zeros_like(qq))


def _online_softmax_step(s, msk, p_v, m_ref, l_ref, acc_ref):
    m = m_ref[...]
    m_new = jnp.maximum(m, jnp.max(s, axis=1, keepdims=True))
    p = jnp.where(msk, jnp.exp(s - m_new), 0.0)
    alpha = jnp.exp(m - m_new)
    l_ref[...] = l_ref[...] * alpha + jnp.sum(p, axis=1, keepdims=True)
    acc_ref[...] = acc_ref[...] * alpha + _dot(p.astype(BF16), p_v)
    m_ref[...] = m_new


def _head_diag_out(acc, l, nh):
    o = acc / l
    return jnp.concatenate([o[h * SAMPLE_ROWS:(h + 1) * SAMPLE_ROWS, h * HEAD_DIM:(h + 1) * HEAD_DIM]
                            for h in range(nh)], axis=1)


def _dsa_sample_attn_kernel(pt_ref, q_ref, key_ref, tau_ref, kp_ref, vp_ref, kn_ref, vn_ref, o_ref,
                            qbd_ref, m_ref, l_ref, acc_ref, *, npg, ls, nh, scale):
    j = pl.program_id(1)
    is_new = j == npg

    @pl.when(j == 0)
    def _():
        qbd_ref[...] = _block_diag_q(q_ref[...], nh)
        m_ref[...] = jnp.full_like(m_ref, NEG_BIG)
        l_ref[...] = jnp.zeros_like(l_ref)
        acc_ref[...] = jnp.zeros_like(acc_ref)

    kp = jnp.where(is_new, kn_ref[...], kp_ref[...]).astype(BF16)
    vp = jnp.where(is_new, vn_ref[...], vp_ref[...]).astype(BF16)
    sel = key_ref[...] >= tau_ref[:, 0:1]
    valid = jnp.logical_or(jnp.logical_not(is_new), _new_key_valid(sel.shape, ls))
    msk = jnp.concatenate([sel & valid] * nh, axis=0)
    s = jnp.where(msk, _dot_nt(qbd_ref[...], kp) * scale, NEG_BIG)
    _online_softmax_step(s, msk, vp, m_ref, l_ref, acc_ref)

    @pl.when(is_new)
    def _():
        o_ref[...] = _head_diag_out(acc_ref[...], l_ref[...], nh).astype(o_ref.dtype)


def _dsa_sample_attn(page_table, q_s, keys, tau, kpool, vpool, k_new, v_new, layer, ls, lay):
    bs, npg = page_table.shape
    page = kpool.shape[2]
    bw, nh = lay.bw, lay.nh
    kern = functools.partial(_dsa_sample_attn_kernel, npg=npg, ls=ls, nh=nh, scale=float(HEAD_DIM ** -0.5))
    pool_spec = pl.BlockSpec((None, None, page, bw),
                             lambda b, j, pt: (layer, pt[b, jnp.minimum(j, npg - 1)], 0, 0))
    new_spec = pl.BlockSpec((None, page, bw), lambda b, j, pt: (b, 0, 0))
    grid_spec = pltpu.PrefetchScalarGridSpec(
        num_scalar_prefetch=1,
        grid=(bs, npg + 1),
        in_specs=[pl.BlockSpec((None, SAMPLE_ROWS, bw), lambda b, j, pt: (b, 0, 0)),
                  pl.BlockSpec((None, SAMPLE_ROWS, page), lambda b, j, pt: (b, 0, j)),
                  pl.BlockSpec((None, SAMPLE_ROWS, LANES), lambda b, j, pt: (b, 0, 0)),
                  pool_spec, pool_spec, new_spec, new_spec],
        out_specs=pl.BlockSpec((None, SAMPLE_ROWS, bw), lambda b, j, pt: (b, 0, 0)),
        scratch_shapes=[pltpu.VMEM((nh * SAMPLE_ROWS, bw), BF16),
                        pltpu.VMEM((nh * SAMPLE_ROWS, 1), F32),
                        pltpu.VMEM((nh * SAMPLE_ROWS, 1), F32),
                        pltpu.VMEM((nh * SAMPLE_ROWS, bw), F32)],
    )
    return pl.pallas_call(
        kern,
        grid_spec=grid_spec,
        out_shape=jax.ShapeDtypeStruct((bs, SAMPLE_ROWS, bw), BF16),
        compiler_params=_params(("parallel", "arbitrary")),
        name="dsa_sample_attn",
    )(page_table, q_s, keys, tau, kpool, vpool, k_new, v_new)


def _upper_ones(n):
    r = lax.broadcasted_iota(I32, (n, n), 0)
    c = lax.broadcasted_iota(I32, (n, n), 1)
    return jnp.where(r <= c, 1.0, 0.0).astype(F32)


def _fox_cumsum_kernel(m2_ref, crow_ref, ccol_ref, carry_ref):
    @pl.when(pl.program_id(0) == 0)
    def _():
        carry_ref[...] = jnp.zeros_like(carry_ref)

    lf_t = m2_ref[...].T
    tc = lf_t.shape[1]
    cs = _dot(lf_t, _upper_ones(tc), HIGHEST) + carry_ref[:, 0:1]
    crow_ref[...] = cs
    ccol_ref[...] = cs.T
    carry_ref[...] = jnp.broadcast_to(cs[:, tc - 1:tc], carry_ref.shape)


def _fox_cumsum(misc2, tp):
    tc = _pick(tp, (512, 256, 128))
    return pl.pallas_call(
        _fox_cumsum_kernel,
        grid=(tp // tc,),
        in_specs=[pl.BlockSpec((tc, LANES), lambda i: (i, 0))],
        out_specs=[pl.BlockSpec((LANES, tc), lambda i: (0, i)), pl.BlockSpec((tc, LANES), lambda i: (i, 0))],
        out_shape=[jax.ShapeDtypeStruct((LANES, tp), F32), jax.ShapeDtypeStruct((tp, LANES), F32)],
        scratch_shapes=[pltpu.VMEM((LANES, LANES), F32)],
        compiler_params=_params(("arbitrary",)),
        name="fox_cumsum",
    )(misc2)


def _fox_prompt_kernel(q_ref, k_ref, v_ref, ccol_ref, crow_ref, o_ref, m_ref, l_ref, acc_ref,
                       *, tq, tk, nh, ff_lane, scale):
    i, j = pl.program_id(0), pl.program_id(1)

    @pl.when(j == 0)
    def _():
        m_ref[...] = jnp.full_like(m_ref, NEG_BIG)
        l_ref[...] = jnp.zeros_like(l_ref)
        acc_ref[...] = jnp.zeros_like(acc_ref)

    @pl.when(j <= i)
    def _():
        causal = (j * tk + lax.broadcasted_iota(I32, (tq, tk), 1)) <= (i * tq + lax.broadcasted_iota(I32, (tq, tk), 0))
        for h in range(nh):
            hs = slice(h * HEAD_DIM, (h + 1) * HEAD_DIM)
            s = _dot_nt(q_ref[:, hs].astype(BF16), k_ref[:, hs].astype(BF16)) * scale
            s = s + ccol_ref[:, ff_lane + h:ff_lane + h + 1] - crow_ref[ff_lane + h:ff_lane + h + 1, :]
            s = jnp.where(causal, s, NEG_BIG)
            m = m_ref[:, h:h + 1]
            m_new = jnp.maximum(m, jnp.max(s, axis=1, keepdims=True))
            p = jnp.where(causal, jnp.exp(s - m_new), 0.0)
            alpha = jnp.exp(m - m_new)
            l_ref[:, h:h + 1] = l_ref[:, h:h + 1] * alpha + jnp.sum(p, axis=1, keepdims=True)
            acc_ref[:, hs] = acc_ref[:, hs] * alpha + _dot(p.astype(BF16), v_ref[:, hs].astype(BF16))
            m_ref[:, h:h + 1] = m_new

    @pl.when(j == i)
    def _():
        outs = [acc_ref[:, h * HEAD_DIM:(h + 1) * HEAD_DIM] / l_ref[:, h:h + 1] for h in range(nh)]
        o_ref[...] = jnp.concatenate(outs, axis=1).astype(o_ref.dtype)


def _fox_prompt(z, ccol, crow, tp, lay):
    bw = lay.bw
    tq = tk = _pick(tp, (512, 256, 128))
    kern = functools.partial(_fox_prompt_kernel, tq=tq, tk=tk, nh=lay.nh, ff_lane=lay.ff_lane,
                             scale=float(HEAD_DIM ** -0.5))
    return pl.pallas_call(
        kern,
        grid=(tp // tq, tp // tk),
        in_specs=[pl.BlockSpec((tq, bw), lambda i, j: (i, lay.fq // bw)),
                  pl.BlockSpec((tk, bw), lambda i, j: (jnp.minimum(j, i), lay.fk // bw)),
                  pl.BlockSpec((tk, bw), lambda i, j: (jnp.minimum(j, i), lay.fv // bw)),
                  pl.BlockSpec((tq, LANES), lambda i, j: (i, 0)),
                  pl.BlockSpec((LANES, tk), lambda i, j: (0, jnp.minimum(j, i)))],
        out_specs=pl.BlockSpec((tq, bw), lambda i, j: (i, 0)),
        out_shape=jax.ShapeDtypeStruct((tp, bw), BF16),
        scratch_shapes=[pltpu.VMEM((tq, LANES), F32), pltpu.VMEM((tq, LANES), F32), pltpu.VMEM((tq, bw), F32)],
        compiler_params=_params(("parallel", "arbitrary")),
        name="fox_prompt",
    )(z, z, z, ccol, crow)


def _fox_sample_kernel(pt_ref, q_ref, kp_ref, vp_ref, lfp_ref, kn_ref, vn_ref, lfn_ref, o_ref,
                       qbd_ref, carry_ref, m_ref, l_ref, acc_ref, *, npg, ls, nh, scale):
    j = pl.program_id(1)
    is_new = j == npg

    @pl.when(j == 0)
    def _():
        qbd_ref[...] = _block_diag_q(q_ref[...], nh)
        carry_ref[...] = jnp.zeros_like(carry_ref)
        m_ref[...] = jnp.full_like(m_ref, NEG_BIG)
        l_ref[...] = jnp.zeros_like(l_ref)
        acc_ref[...] = jnp.zeros_like(acc_ref)

    kp = jnp.where(is_new, kn_ref[...], kp_ref[...]).astype(BF16)
    vp = jnp.where(is_new, vn_ref[...], vp_ref[...]).astype(BF16)
    lf = jnp.where(is_new, lfn_ref[...], lfp_ref[...])
    page = lf.shape[1]
    cs = _dot(lf, _upper_ones(page), HIGHEST) + carry_ref[:, 0:1]
    carry_ref[...] = jnp.broadcast_to(cs[:, page - 1:page], carry_ref.shape)
    bias = jnp.concatenate([jnp.broadcast_to(cs[h:h + 1, :], (SAMPLE_ROWS, page)) for h in range(nh)], axis=0)
    valid8 = jnp.logical_or(jnp.logical_not(is_new), _new_key_valid((SAMPLE_ROWS, page), ls))
    msk = jnp.concatenate([valid8] * nh, axis=0)
    s = jnp.where(msk, _dot_nt(qbd_ref[...], kp) * scale - bias, NEG_BIG)
    _online_softmax_step(s, msk, vp, m_ref, l_ref, acc_ref)

    @pl.when(is_new)
    def _():
        o_ref[...] = _head_diag_out(acc_ref[...], l_ref[...], nh).astype(o_ref.dtype)


def _fox_sample(page_table, q_s, kpool, vpool, lfpool_t, k_new, v_new, lf_new, layer, ls, lay):
    bs, npg = page_table.shape
    page = kpool.shape[2]
    bw, nh = lay.bw, lay.nh
    kern = functools.partial(_fox_sample_kernel, npg=npg, ls=ls, nh=nh, scale=float(HEAD_DIM ** -0.5))
    pidx = lambda b, j, pt: (layer, pt[b, jnp.minimum(j, npg - 1)], 0, 0)
    pool_spec = pl.BlockSpec((None, None, page, bw), pidx)
    new_spec = pl.BlockSpec((None, page, bw), lambda b, j, pt: (b, 0, 0))
    grid_spec = pltpu.PrefetchScalarGridSpec(
        num_scalar_prefetch=1,
        grid=(bs, npg + 1),
        in_specs=[pl.BlockSpec((None, SAMPLE_ROWS, bw), lambda b, j, pt: (b, 0, 0)),
                  pool_spec, pool_spec,
                  pl.BlockSpec((None, None, SAMPLE_ROWS, page), pidx),
                  new_spec, new_spec,
                  pl.BlockSpec((None, SAMPLE_ROWS, page), lambda b, j, pt: (b, 0, 0))],
        out_specs=pl.BlockSpec((None, SAMPLE_ROWS, bw), lambda b, j, pt: (b, 0, 0)),
        scratch_shapes=[pltpu.VMEM((nh * SAMPLE_ROWS, bw), BF16),
                        pltpu.VMEM((SAMPLE_ROWS, LANES), F32),
                        pltpu.VMEM((nh * SAMPLE_ROWS, 1), F32),
                        pltpu.VMEM((nh * SAMPLE_ROWS, 1), F32),
                        pltpu.VMEM((nh * SAMPLE_ROWS, bw), F32)],
    )
    return pl.pallas_call(
        kern,
        grid_spec=grid_spec,
        out_shape=jax.ShapeDtypeStruct((bs, SAMPLE_ROWS, bw), BF16),
        compiler_params=_params(("parallel", "arbitrary")),
        name="fox_sample",
    )(page_table, q_s, kpool, vpool, lfpool_t, k_new, v_new, lf_new)


def _hgrn_lower_bound(lg_ref, layer):
    lg = lg_ref[...]
    e = jnp.exp(lg - jnp.max(lg, axis=0, keepdims=True))
    p = e / jnp.sum(e, axis=0, keepdims=True)
    lb = jnp.zeros((1, lg.shape[1]), F32)
    for i in range(1, layer + 1):
        lb = lb + p[i:i + 1, :]
    return lb


def _hgrn_rows(hq, hf, hi, lb, chunk, row_valid, nh):
    r = hq.shape[0]
    pos = lax.broadcasted_iota(I32, hq.shape, 0) & (chunk - 1)
    lf = jnp.log(lb + (1.0 - lb) * jax.nn.sigmoid(hf))
    kk = (1.0 - lb) * jax.nn.sigmoid(-hf)
    if row_valid is not None:
        lf = jnp.where(row_valid, lf, 0.0)
        kk = jnp.where(row_valid, kk, 0.0)
    b = lf
    step = 1
    while step < chunk:
        b = b + jnp.where(pos >= step, pltpu.roll(b, step, 0), 0.0)
        step *= 2
    suf = lf
    step = 1
    while step < chunk:
        suf = suf + jnp.where(pos + step < chunk, pltpu.roll(suf, r - step, 0), 0.0)
        step *= 2
    tail = suf - lf
    qe = hq * jnp.exp(b)
    kd = kk * jnp.exp(tail)
    etot = jnp.exp(b + tail)
    o_intra = jnp.zeros_like(hq)
    for d in range(chunk):
        if d == 0:
            kk_d, b_d, i_d = kk, b, hi
        else:
            kk_d, b_d, i_d = pltpu.roll(kk, d, 0), pltpu.roll(b, d, 0), pltpu.roll(hi, d, 0)
        dec = jnp.exp(jnp.where(pos >= d, b - b_d, NEG_BIG))
        prod = hq * dec * kk_d
        a = jnp.concatenate(
            [jnp.broadcast_to(jnp.sum(prod[:, h * HEAD_DIM:(h + 1) * HEAD_DIM], axis=1, keepdims=True), (r, HEAD_DIM))
             for h in range(nh)], axis=1)
        o_intra = o_intra + a * i_d
    return qe, kd, etot, o_intra


def _hgrn_finish(o, hg, ng, nh):
    outs = []
    for h in range(nh):
        oh = o[:, h * HEAD_DIM:(h + 1) * HEAD_DIM]
        outs.append(oh * lax.rsqrt(jnp.mean(oh * oh, axis=1, keepdims=True) + EPS))
    return jnp.concatenate(outs, axis=1) * ng * (hg * jax.nn.sigmoid(hg))


def _hgrn_prompt_kernel(hq_ref, hf_ref, hi_ref, hg_ref, lg_ref, ng_ref, y_ref, s_ref,
                        st_ref, qe_ref, kd_ref, et_ref, oi_ref, *, layer, chunk, nh):
    @pl.when(pl.program_id(0) == 0)
    def _():
        st_ref[...] = jnp.zeros_like(st_ref)

    lb = _hgrn_lower_bound(lg_ref, layer)
    hi = hi_ref[...]
    qe, kd, etot, o_intra = _hgrn_rows(hq_ref[...], hf_ref[...], hi, lb, chunk, None, nh)
    qe_ref[...] = qe.astype(BF16)
    kd_ref[...] = kd.astype(BF16)
    et_ref[...] = etot
    tm = hi.shape[0]

    def chunk_body(c, carry):
        rows = pl.ds(pl.multiple_of(c * chunk, chunk), chunk)
        for h in range(nh):
            hs = slice(h * HEAD_DIM, (h + 1) * HEAD_DIM)
            st = st_ref[h]
            oi_ref[rows, hs] = _dot_nt(qe_ref[rows, hs], st.astype(BF16))
            e = et_ref[rows, hs]
            st_ref[h] = st * e[0:1, :] + _dot_tn(hi_ref[rows, hs].astype(BF16), kd_ref[rows, hs])
        return carry

    lax.fori_loop(0, tm // chunk, chunk_body, 0)
    y_ref[...] = _hgrn_finish(oi_ref[...] + o_intra, hg_ref[...], ng_ref[...], nh).astype(y_ref.dtype)

    @pl.when(pl.program_id(0) == pl.num_programs(0) - 1)
    def _():
        for h in range(nh):
            s_ref[h] = st_ref[h].T


def _hgrn_prompt(z, lb_logits, norm_g, tp, lay, layer):
    bw, nh = lay.bw, lay.nh
    tm = _pick(tp, (512, 256, 128))
    dep = lb_logits.shape[0]
    col = lambda off: pl.BlockSpec((tm, bw), lambda i: (i, off // bw))
    kern = functools.partial(_hgrn_prompt_kernel, layer=layer, chunk=HG_CHUNK_PROMPT, nh=nh)
    return pl.pallas_call(
        kern,
        grid=(tp // tm,),
        in_specs=[col(lay.hq), col(lay.hf), col(lay.hi), col(lay.hg),
                  pl.BlockSpec((dep, bw), lambda i: (0, 0)),
                  pl.BlockSpec((None, 1, bw), lambda i: (layer, 0, 0))],
        out_specs=[pl.BlockSpec((tm, bw), lambda i: (i, 0)),
                   pl.BlockSpec((nh, HEAD_DIM, HEAD_DIM), lambda i: (0, 0, 0))],
        out_shape=[jax.ShapeDtypeStruct((tp, bw), BF16), jax.ShapeDtypeStruct((nh, HEAD_DIM, HEAD_DIM), F32)],
        scratch_shapes=[pltpu.VMEM((nh, HEAD_DIM, HEAD_DIM), F32), pltpu.VMEM((tm, bw), BF16),
                        pltpu.VMEM((tm, bw), BF16), pltpu.VMEM((tm, bw), F32), pltpu.VMEM((tm, bw), F32)],
        compiler_params=_params(("arbitrary",)),
        name="hgrn_prompt",
    )(z, z, z, z, lb_logits, norm_g)


def _hgrn_sample_kernel(hq_ref, hf_ref, hi_ref, hg_ref, lg_ref, ng_ref, s0_ref, y_ref, s_ref,
                        *, layer, ls, nh):
    lb = _hgrn_lower_bound(lg_ref, layer)
    hi = hi_ref[...]
    row_valid = lax.broadcasted_iota(I32, hi.shape, 0) < ls
    hi = jnp.where(row_valid, hi, 0.0)
    qe, kd, etot, o_intra = _hgrn_rows(hq_ref[...], hf_ref[...], hi, lb, SAMPLE_ROWS, row_valid, nh)
    outs = []
    for h in range(nh):
        hs = slice(h * HEAD_DIM, (h + 1) * HEAD_DIM)
        st = s0_ref[h].T
        outs.append(_dot_nt(qe[:, hs].astype(BF16), st.astype(BF16)))
        st = st * etot[0:1, hs] + _dot_tn(hi[:, hs].astype(BF16), kd[:, hs].astype(BF16))
        s_ref[h] = st.T
    o = jnp.concatenate(outs, axis=1) + o_intra
    y_ref[...] = _hgrn_finish(o, hg_ref[...], ng_ref[...], nh).astype(y_ref.dtype)


def _hgrn_sample(hq, hf, hi, hg, lb_logits, norm_g, state, layer, ls, lay):
    bs = hq.shape[0]
    bw, nh = lay.bw, lay.nh
    dep = lb_logits.shape[0]
    rows = pl.BlockSpec((None, SAMPLE_ROWS, bw), lambda b: (b, 0, 0))
    kern = functools.partial(_hgrn_sample_kernel, layer=layer, ls=ls, nh=nh)
    return pl.pallas_call(
        kern,
        grid=(bs,),
        in_specs=[rows, rows, rows, rows,
                  pl.BlockSpec((dep, bw), lambda b: (0, 0)),
                  pl.BlockSpec((None, 1, bw), lambda b: (layer, 0, 0)),
                  pl.BlockSpec((None, None, nh, HEAD_DIM, HEAD_DIM), lambda b: (layer, b, 0, 0, 0))],
        out_specs=[rows, pl.BlockSpec((None, nh, HEAD_DIM, HEAD_DIM), lambda b: (b, 0, 0, 0))],
        out_shape=[jax.ShapeDtypeStruct((bs, SAMPLE_ROWS, bw), BF16),
                   jax.ShapeDtypeStruct((bs, nh, HEAD_DIM, HEAD_DIM), F32)],
        compiler_params=_params(("parallel",)),
        name="hgrn_sample",
    )(hq, hf, hi, hg, lb_logits, norm_g, state)


def _pad_rows(a, bs, ls, rows):
    return jnp.pad(a.reshape(bs, ls, a.shape[-1]), ((0, 0), (0, rows - ls), (0, 0)))


def kernel(x_prompt, x_sample, cache_dsa_k, cache_dsa_v, cache_dsa_kidx, cache_fox_k, cache_fox_v,
           cache_fox_logf, state_conv, state_hgrn, page_table, norm_mix_g, w_in, conv_w, fox_fb,
           hgrn_lb_logits, hgrn_norm_g, w_branch, w_out, norm_ffn_g, w_up, w_down, final_norm_g):
    bp, tp, d = x_prompt.shape
    bs, ls, _ = x_sample.shape
    assert bp == 1 and ls <= SAMPLE_ROWS
    depth = w_in.shape[0]
    lay = _Layout(d)
    bw, nh = lay.bw, lay.nh
    ts = bs * ls
    npg = page_table.shape[1]
    n_pool, page = cache_dsa_k.shape[1], cache_dsa_k.shape[2]
    assert page == LANES and tp % LANES == 0
    past = npg * page

    pos = jnp.concatenate([jnp.arange(tp), past + jnp.tile(jnp.arange(ls), bs)])
    tab_a, half_a = _rope_tables(pos, HEAD_DIM)
    tab_i, half_i = _rope_tables(pos, IDX_DIM)

    w_in_r = _relayout_w_in(w_in, lay)
    w_br_b, w_out_b = w_branch.astype(BF16), w_out.astype(BF16)
    w_up_b, w_down_b = w_up.astype(BF16), w_down.astype(BF16)
    fb_rows = jnp.pad(fox_fb, ((0, 0), (lay.ff_lane, LANES - lay.ff_lane - nh)))[:, None, :]
    g_mix, g_ffn, g_hg = norm_mix_g[:, None, :], norm_ffn_g[:, None, :], hgrn_norm_g[:, None, :]
    kpool_a = cache_dsa_k.reshape(depth, n_pool, page, bw)
    vpool_a = cache_dsa_v.reshape(depth, n_pool, page, bw)
    kpool_f = cache_fox_k.reshape(depth, n_pool, page, bw)
    vpool_f = cache_fox_v.reshape(depth, n_pool, page, bw)
    lfpool_t = jnp.pad(jnp.swapaxes(cache_fox_logf, 2, 3), ((0, 0), (0, 0), (0, SAMPLE_ROWS - nh), (0, 0)))
    conv_state = state_conv.reshape(depth, bs, (CONV_W - 1) * bw)

    x = jnp.concatenate([x_prompt[0], x_sample.reshape(ts, d)], axis=0)
    new_p, new_s = [], []
    for l in range(depth):
        z = _in_proj(x, g_mix, w_in_r, l)
        aq_r, ak_r, ak_b, av_b, iq_r, ik_r, ik_b, misc2 = _post(z, tab_a, tab_i, fb_rows, lay, l, half_a, half_i)
        zs = z[tp:]
        sec = lambda off: zs[:, off:off + bw]

        y_conv_p, u_last = _conv_prompt(z, conv_w, tp, lay, l)
        flat = lambda a: a.reshape(bs, ls * bw)
        y_conv_s, conv_new_s = _conv_sample(flat(sec(lay.ch)), flat(sec(lay.cb)), flat(sec(lay.cc)),
                                            conv_state, conv_w, l, bs, ls, bw)

        y_dsa_p = _dsa_prompt(iq_r, misc2, aq_r, ik_b, ak_b, av_b, tp, lay)
        iq_s = _pad_rows(iq_r[tp:], bs, ls, SAMPLE_ROWS).reshape(bs, SAMPLE_ROWS, IDX_HEADS, LANES)
        iq_s = jnp.swapaxes(iq_s, 1, 2).reshape(bs, IDX_HEADS * SAMPLE_ROWS, LANES)
        iw_s = jnp.swapaxes(_pad_rows(misc2[tp:, :IDX_HEADS], bs, ls, SAMPLE_ROWS), 1, 2)
        iw_s = jnp.broadcast_to(iw_s.reshape(bs, IDX_HEADS * SAMPLE_ROWS, 1), (bs, IDX_HEADS * SAMPLE_ROWS, LANES))
        ik_new = _pad_rows(ik_r[tp:, :IDX_DIM], bs, ls, page)
        keys, tau = _dsa_sample_score(page_table, iq_s, iw_s, cache_dsa_kidx, ik_new, l, ls)
        y_dsa_s = _dsa_sample_attn(page_table, _pad_rows(aq_r[tp:], bs, ls, SAMPLE_ROWS), keys, tau,
                                   kpool_a, vpool_a, _pad_rows(ak_r[tp:], bs, ls, page),
                                   _pad_rows(sec(lay.av), bs, ls, page), l, ls, lay)

        crow, ccol = _fox_cumsum(misc2, tp)
        y_fox_p = _fox_prompt(z, ccol, crow, tp, lay)
        logf_s = misc2[tp:, lay.ff_lane:lay.ff_lane + nh]
        lf_new = jnp.pad(jnp.swapaxes(logf_s.reshape(bs, ls, nh), 1, 2),
                         ((0, 0), (0, SAMPLE_ROWS - nh), (0, page - ls)))
        y_fox_s = _fox_sample(page_table, _pad_rows(sec(lay.fq), bs, ls, SAMPLE_ROWS), kpool_f, vpool_f, lfpool_t,
                              _pad_rows(sec(lay.fk), bs, ls, page), _pad_rows(sec(lay.fv), bs, ls, page),
                              lf_new, l, ls, lay)

        y_hg_p, s_hg_p = _hgrn_prompt(z, hgrn_lb_logits, g_hg, tp, lay, l)
        pad8 = lambda off: _pad_rows(sec(off), bs, ls, SAMPLE_ROWS)
        y_hg_s, s_hg_s = _hgrn_sample(pad8(lay.hq), pad8(lay.hf), pad8(lay.hi), pad8(lay.hg),
                                      hgrn_lb_logits, g_hg, state_hgrn, l, ls, lay)

        unpad = lambda a: a[:, :ls].reshape(ts, bw)
        branches = [jnp.concatenate([y_conv_p, y_conv_s.reshape(ts, bw)], axis=0),
                    jnp.concatenate([y_dsa_p, unpad(y_dsa_s)], axis=0),
                    jnp.concatenate([y_fox_p, unpad(y_fox_s)], axis=0),
                    jnp.concatenate([y_hg_p, unpad(y_hg_s)], axis=0)]
        x = _mix(branches, z, w_br_b, w_out_b, x, lay, l)
        x = _ffn(x, g_ffn, w_up_b, w_down_b, l)

        heads = lambda a, b_, l_: a.reshape(b_, l_, nh, HEAD_DIM)
        zp = lambda off: z[:tp, off:off + bw]
        new_p.append((heads(ak_r[:tp], 1, tp), heads(zp(lay.av), 1, tp), ik_r[:tp, :IDX_DIM].reshape(1, tp, IDX_DIM),
                      heads(zp(lay.fk), 1, tp), heads(zp(lay.fv), 1, tp),
                      misc2[:tp, lay.ff_lane:lay.ff_lane + nh].reshape(1, tp, nh),
                      u_last[8 - (CONV_W - 1):].reshape(1, CONV_W - 1, bw), s_hg_p[None]))
        new_s.append((heads(ak_r[tp:], bs, ls), heads(sec(lay.av), bs, ls), ik_r[tp:, :IDX_DIM].reshape(bs, ls, IDX_DIM),
                      heads(sec(lay.fk), bs, ls), heads(sec(lay.fv), bs, ls), logf_s.reshape(bs, ls, nh),
                      conv_new_s.reshape(bs, CONV_W - 1, bw), s_hg_s))

    y = _final_norm(x, final_norm_g[None, :])
    y_prompt = y[:tp][None]
    y_sample = y[tp:].reshape(bs, ls, d)
    outs_p = [jnp.stack(t) for t in zip(*new_p)]
    outs_s = [jnp.stack(t) for t in zip(*new_s)]
    return (y_prompt, y_sample, *outs_p, *outs_s)
```

```python
import functools

import jax
import jax.numpy as jnp
from jax import lax
from jax.experimental import pallas as pl
from jax.experimental.pallas import tpu as pltpu

F32 = jnp.float32
BF16 = jnp.bfloat16
I32 = jnp.int32

LANES = 128
HEAD_DIM = 128
IDX_DIM = 64
IDX_HEADS = 16
CONV_W = 3
TOPK_MAX = 256
ROPE_THETA = 500000.0
ROT_FRACTION = 4
EPS = 1e-6
NEG_BIG = -1e30
INT_MIN = -2 ** 31
SAMPLE_ROWS = 8
HG_CHUNK_PROMPT = 16
BISECT_STEPS_PER_CHECK = 4
VMEM_LIMIT = 56 * 1024 * 1024
HIGHEST = lax.Precision.HIGHEST


def _dot(a, b, precision=None):
    return jnp.dot(a, b, preferred_element_type=F32, precision=precision)


def _dot_nt(a, b):
    return lax.dot_general(a, b, (((1,), (1,)), ((), ())), preferred_element_type=F32)


def _dot_tn(a, b):
    return lax.dot_general(a, b, (((0,), (0,)), ((), ())), preferred_element_type=F32)


def _pick(n, cands):
    for c in cands:
        if n % c == 0:
            return c
    raise ValueError(f"no tile for {n} in {cands}")


def _params(sem, vmem=VMEM_LIMIT):
    return pltpu.CompilerParams(dimension_semantics=sem, vmem_limit_bytes=vmem)


def _log_sigmoid(x):
    return jnp.minimum(x, 0.0) - jnp.log1p(jnp.exp(-jnp.abs(x)))


def _rms(x, g):
    return x * lax.rsqrt(jnp.mean(x * x, axis=-1, keepdims=True) + EPS) * g


class _Layout:
    def __init__(self, d_model):
        bw = d_model // 4
        assert bw % HEAD_DIM == 0
        self.d = d_model
        self.bw = bw
        self.nh = bw // HEAD_DIM
        self.iqw = IDX_HEADS * LANES
        assert self.iqw % bw == 0 and IDX_HEADS + self.nh <= LANES
        o = 0
        for name in ("ch", "cb", "cc", "aq", "ak", "av", "fq", "fk"):
            setattr(self, name, o)
            o += bw
        self.iq = o
        o += self.iqw
        for name in ("fv", "hq", "hf", "hi", "hg"):
            setattr(self, name, o)
            o += bw
        self.ikp = o
        o += LANES
        self.misc = o
        o += LANES
        o = -(-o // bw) * bw
        self.gate = o
        o += 4 * d_model
        self.n = o
        self.ff_lane = IDX_HEADS


def _relayout_w_in(w_in, lay):
    bw, nh = lay.bw, lay.nh
    dep, d, _ = w_in.shape
    o = 0

    def take(width):
        nonlocal o
        s = w_in[:, :, o:o + width]
        o += width
        return s

    first6 = take(6 * bw)
    iq = take(IDX_HEADS * IDX_DIM)
    ik = take(IDX_DIM)
    iw = take(IDX_HEADS)
    fq, fk, fv = take(bw), take(bw), take(bw)
    ff = take(nh)
    h4 = take(4 * bw)
    gate = take(4 * lay.d)
    assert o == w_in.shape[-1]
    iq_pad = jnp.pad(iq.reshape(dep, d, IDX_HEADS, IDX_DIM),
                     ((0, 0), (0, 0), (0, 0), (0, LANES - IDX_DIM))).reshape(dep, d, lay.iqw)
    ik_pad = jnp.pad(ik, ((0, 0), (0, 0), (0, LANES - IDX_DIM)))
    misc = jnp.pad(jnp.concatenate([iw, ff], axis=-1), ((0, 0), (0, 0), (0, LANES - IDX_HEADS - nh)))
    pad = jnp.zeros((dep, d, lay.gate - lay.misc - LANES), w_in.dtype)
    out = jnp.concatenate([first6, fq, fk, iq_pad, fv, h4, ik_pad, misc, pad, gate], axis=-1)
    assert out.shape[-1] == lay.n
    return out.astype(BF16)


def _rope_tables(pos, head_dim):
    d_rot = head_dim // ROT_FRACTION
    half = d_rot // 2
    inv = ROPE_THETA ** (-jnp.arange(half, dtype=F32) / half)
    ang = pos.astype(F32)[:, None] * inv[None, :]
    cos, sin = jnp.cos(ang), jnp.sin(ang)
    t = pos.shape[0]
    rest = LANES - d_rot
    c = jnp.concatenate([cos, cos, jnp.ones((t, rest), F32)], axis=1)
    s1 = jnp.concatenate([-sin, jnp.zeros((t, half + rest), F32)], axis=1)
    s2 = jnp.concatenate([jnp.zeros((t, half), F32), sin, jnp.zeros((t, rest), F32)], axis=1)
    return jnp.concatenate([c, s1, s2], axis=1), half


def _in_proj_kernel(x_ref, g_ref, w_ref, z_ref, xn_ref):
    @pl.when(pl.program_id(1) == 0)
    def _():
        xn_ref[...] = _rms(x_ref[...], g_ref[...]).astype(BF16)

    z_ref[...] = _dot(xn_ref[...], w_ref[...])


def _in_proj(x, g, w_r, layer):
    t, d = x.shape
    n = w_r.shape[-1]
    tm = _pick(t, (640, 512, 256, 128))
    tn = _pick(n, (1024, 512))
    return pl.pallas_call(
        _in_proj_kernel,
        grid=(t // tm, n // tn),
        in_specs=[pl.BlockSpec((tm, d), lambda i, j: (i, 0)),
                  pl.BlockSpec((None, 1, d), lambda i, j: (layer, 0, 0)),
                  pl.BlockSpec((None, d, tn), lambda i, j: (layer, 0, j))],
        out_specs=pl.BlockSpec((tm, tn), lambda i, j: (i, j)),
        out_shape=jax.ShapeDtypeStruct((t, n), F32),
        scratch_shapes=[pltpu.VMEM((tm, d), BF16)],
        compiler_params=_params(("parallel", "arbitrary")),
        name="in_proj",
    )(x, g, w_r)


def _rope(x, tab, half, reps):
    w = x.shape[1]
    c = jnp.concatenate([tab[:, 0:LANES]] * reps, axis=1)
    s1 = jnp.concatenate([tab[:, LANES:2 * LANES]] * reps, axis=1)
    s2 = jnp.concatenate([tab[:, 2 * LANES:3 * LANES]] * reps, axis=1)
    return x * c + pltpu.roll(x, w - half, 1) * s1 + pltpu.roll(x, half, 1) * s2


def _post_kernel(aq_ref, ak_ref, av_ref, iq_ref, ikp_ref, misc_ref, fq_ref, fk_ref, fv_ref,
                 ta_ref, ti_ref, fb_ref,
                 aqr_ref, akr_ref, akb_ref, avb_ref, iqr_ref, ikr_ref, ikb_ref, m2_ref,
                 fqb_ref, fkb_ref, fvb_ref, *, nh, half_a, half_i, idx_scale):
    ta, ti = ta_ref[...], ti_ref[...]
    fqb_ref[...] = fq_ref[...].astype(BF16)
    fkb_ref[...] = fk_ref[...].astype(BF16)
    fvb_ref[...] = fv_ref[...].astype(BF16)
    aqr_ref[...] = _rope(aq_ref[...], ta, half_a, nh).astype(BF16)
    akr = _rope(ak_ref[...], ta, half_a, nh)
    akr_ref[...] = akr
    akb_ref[...] = akr.astype(BF16)
    avb_ref[...] = av_ref[...].astype(BF16)
    iqr_ref[...] = _rope(iq_ref[...], ti, half_i, IDX_HEADS).astype(BF16)
    ikr = _rope(ikp_ref[...], ti, half_i, 1)
    ikr_ref[...] = ikr
    ikb_ref[...] = ikr.astype(BF16)
    m = misc_ref[...]
    lane = lax.broadcasted_iota(I32, m.shape, 1)
    logf = _log_sigmoid(m + fb_ref[...])
    m2_ref[...] = jnp.where(lane < IDX_HEADS, m * idx_scale,
                            jnp.where(lane < IDX_HEADS + nh, logf, 0.0))


def _post(z, tab_a, tab_i, fb_row, lay, layer, half_a, half_i):
    t = z.shape[0]
    bw = lay.bw
    tm = _pick(t, (320, 256, 128, 64, 32, 16))
    col = lambda off, w: pl.BlockSpec((tm, w), lambda i: (i, off // w))
    row = lambda w: pl.BlockSpec((tm, w), lambda i: (i, 0))
    kern = functools.partial(_post_kernel, nh=lay.nh, half_a=half_a, half_i=half_i,
                             idx_scale=float((IDX_HEADS * IDX_DIM) ** -0.5))
    return pl.pallas_call(
        kern,
        grid=(t // tm,),
        in_specs=[col(lay.aq, bw), col(lay.ak, bw), col(lay.av, bw), col(lay.iq, lay.iqw),
                  col(lay.ikp, LANES), col(lay.misc, LANES), col(lay.fq, bw), col(lay.fk, bw), col(lay.fv, bw),
                  row(3 * LANES), row(3 * LANES),
                  pl.BlockSpec((None, 1, LANES), lambda i: (layer, 0, 0))],
        out_specs=[row(bw), row(bw), row(bw), row(bw), row(lay.iqw), row(LANES), row(LANES), row(LANES),
                   row(bw), row(bw), row(bw)],
        out_shape=[jax.ShapeDtypeStruct((t, bw), BF16), jax.ShapeDtypeStruct((t, bw), F32),
                   jax.ShapeDtypeStruct((t, bw), BF16), jax.ShapeDtypeStruct((t, bw), BF16),
                   jax.ShapeDtypeStruct((t, lay.iqw), BF16), jax.ShapeDtypeStruct((t, LANES), F32),
                   jax.ShapeDtypeStruct((t, LANES), BF16), jax.ShapeDtypeStruct((t, LANES), F32),
                   jax.ShapeDtypeStruct((t, bw), BF16), jax.ShapeDtypeStruct((t, bw), BF16),
                   jax.ShapeDtypeStruct((t, bw), BF16)],
        compiler_params=_params(("parallel",)),
        name="post_proj",
    )(z, z, z, z, z, z, z, z, z, tab_a, tab_i, fb_row)


def _mix_kernel(b0_ref, b1_ref, b2_ref, b3_ref, g0_ref, g1_ref, g2_ref, g3_ref,
                wbr_ref, wout_ref, x_ref, o_ref, acc_ref):
    dt = pl.program_id(1)

    @pl.when(dt == 0)
    def _():
        acc_ref[...] = jnp.zeros_like(acc_ref)

    m = None
    for n, (b_ref, g_ref) in enumerate(((b0_ref, g0_ref), (b1_ref, g1_ref), (b2_ref, g2_ref), (b3_ref, g3_ref))):
        term = jax.nn.sigmoid(g_ref[...]) * _dot(b_ref[...], wbr_ref[n])
        m = term if m is None else m + term
    acc_ref[...] += _dot(m.astype(BF16), wout_ref[...])

    @pl.when(dt == pl.num_programs(1) - 1)
    def _():
        o_ref[...] = x_ref[...] + acc_ref[...]


def _mix(branches, z, w_br, w_out, x, lay, layer):
    t, d = x.shape
    bw = lay.bw
    tm = _pick(t, (640, 512, 256, 128))
    td = 512
    assert d % td == 0 and lay.gate % td == 0
    gspec = lambda n: pl.BlockSpec((tm, td), lambda i, j: (i, (lay.gate + n * d) // td + j))
    bspec = pl.BlockSpec((tm, bw), lambda i, j: (i, 0))
    return pl.pallas_call(
        _mix_kernel,
        grid=(t // tm, d // td),
        in_specs=[bspec, bspec, bspec, bspec, gspec(0), gspec(1), gspec(2), gspec(3),
                  pl.BlockSpec((None, 4, bw, td), lambda i, j: (layer, 0, 0, j)),
                  pl.BlockSpec((None, td, d), lambda i, j: (layer, j, 0)),
                  pl.BlockSpec((tm, d), lambda i, j: (i, 0))],
        out_specs=pl.BlockSpec((tm, d), lambda i, j: (i, 0)),
        out_shape=jax.ShapeDtypeStruct((t, d), F32),
        scratch_shapes=[pltpu.VMEM((tm, d), F32)],
        compiler_params=_params(("parallel", "arbitrary")),
        name="branch_mix",
    )(*branches, z, z, z, z, w_br, w_out, x)


def _ffn_kernel(x_ref, g_ref, wu_ref, wd_ref, o_ref, xn_ref, acc_ref):
    f = pl.program_id(1)

    @pl.when(f == 0)
    def _():
        xn_ref[...] = _rms(x_ref[...], g_ref[...]).astype(BF16)
        acc_ref[...] = jnp.zeros_like(acc_ref)

    h = jnp.maximum(_dot(xn_ref[...], wu_ref[...]), 0.0)
    acc_ref[...] += _dot((h * h).astype(BF16), wd_ref[...])

    @pl.when(f == pl.num_programs(1) - 1)
    def _():
        o_ref[...] = x_ref[...] + acc_ref[...]


def _ffn(x, g, w_up, w_down, layer):
    t, d = x.shape
    dff = w_up.shape[-1]
    tm = _pick(t, (640, 512, 256, 128))
    tf = _pick(dff, (512, 256, 128))
    return pl.pallas_call(
        _ffn_kernel,
        grid=(t // tm, dff // tf),
        in_specs=[pl.BlockSpec((tm, d), lambda i, j: (i, 0)),
                  pl.BlockSpec((None, 1, d), lambda i, j: (layer, 0, 0)),
                  pl.BlockSpec((None, d, tf), lambda i, j: (layer, 0, j)),
                  pl.BlockSpec((None, tf, d), lambda i, j: (layer, j, 0))],
        out_specs=pl.BlockSpec((tm, d), lambda i, j: (i, 0)),
        out_shape=jax.ShapeDtypeStruct((t, d), F32),
        scratch_shapes=[pltpu.VMEM((tm, d), BF16), pltpu.VMEM((tm, d), F32)],
        compiler_params=_params(("parallel", "arbitrary")),
        name="ffn",
    )(x, g, w_up, w_down)


def _final_norm_kernel(x_ref, g_ref, o_ref):
    o_ref[...] = _rms(x_ref[...], g_ref[...])


def _final_norm(x, g):
    t, d = x.shape
    tm = _pick(t, (640, 512, 256, 128))
    return pl.pallas_call(
        _final_norm_kernel,
        grid=(t // tm,),
        in_specs=[pl.BlockSpec((tm, d), lambda i: (i, 0)), pl.BlockSpec((1, d), lambda i: (0, 0))],
        out_specs=pl.BlockSpec((tm, d), lambda i: (i, 0)),
        out_shape=jax.ShapeDtypeStruct((t, d), F32),
        compiler_params=_params(("parallel",)),
        name="final_norm",
    )(x, g)


def _conv_prompt_kernel(ch_ref, cb_ref, cc_ref, w_ref, y_ref, last_ref, carry_ref):
    @pl.when(pl.program_id(0) == 0)
    def _():
        carry_ref[...] = jnp.zeros_like(carry_ref)

    u = cc_ref[...] * ch_ref[...]
    tm = u.shape[0]
    rid = lax.broadcasted_iota(I32, u.shape, 0)
    prev = carry_ref[...]
    acc = w_ref[CONV_W - 1:CONV_W, :] * u
    for back in range(1, CONV_W):
        head = jnp.concatenate([pltpu.roll(prev, back, 0), jnp.zeros((tm - 8, u.shape[1]), F32)], axis=0)
        shifted = jnp.where(rid < back, head, pltpu.roll(u, back, 0))
        acc = acc + w_ref[CONV_W - 1 - back:CONV_W - back, :] * shifted
    y_ref[...] = (cb_ref[...] * acc).astype(y_ref.dtype)
    carry_ref[...] = u[tm - 8:tm, :]
    last_ref[...] = u[tm - 8:tm, :]


def _conv_prompt(z, conv_w, tp, lay, layer):
    bw = lay.bw
    tm = _pick(tp, (512, 256, 128))
    col = lambda off: pl.BlockSpec((tm, bw), lambda i: (i, off // bw))
    return pl.pallas_call(
        _conv_prompt_kernel,
        grid=(tp // tm,),
        in_specs=[col(lay.ch), col(lay.cb), col(lay.cc),
                  pl.BlockSpec((None, CONV_W, bw), lambda i: (layer, 0, 0))],
        out_specs=[pl.BlockSpec((tm, bw), lambda i: (i, 0)), pl.BlockSpec((8, bw), lambda i: (0, 0))],
        out_shape=[jax.ShapeDtypeStruct((tp, bw), BF16), jax.ShapeDtypeStruct((8, bw), F32)],
        scratch_shapes=[pltpu.VMEM((8, bw), F32)],
        compiler_params=_params(("arbitrary",)),
        name="conv_prompt",
    )(z, z, z, conv_w)


def _conv_sample_kernel(ch_ref, cb_ref, cc_ref, st_ref, w_ref, y_ref, ns_ref, *, ls, bw):
    u_ext = jnp.concatenate([st_ref[...], cc_ref[...] * ch_ref[...]], axis=1)
    cb = cb_ref[...]
    ys = []
    for t in range(ls):
        acc = None
        for j in range(CONV_W):
            term = w_ref[j:j + 1, :] * u_ext[:, (t + j) * bw:(t + j + 1) * bw]
            acc = term if acc is None else acc + term
        ys.append(cb[:, t * bw:(t + 1) * bw] * acc)
    y_ref[...] = jnp.concatenate(ys, axis=1).astype(y_ref.dtype)
    ns_ref[...] = u_ext[:, ls * bw:(ls + CONV_W - 1) * bw]


def _conv_sample(ch, cb, cc, state, conv_w, layer, bs, ls, bw):
    kern = functools.partial(_conv_sample_kernel, ls=ls, bw=bw)
    full = lambda shape: pl.BlockSpec(shape, lambda i: (0,) * len(shape))
    return pl.pallas_call(
        kern,
        grid=(1,),
        in_specs=[full((bs, ls * bw)), full((bs, ls * bw)), full((bs, ls * bw)),
                  pl.BlockSpec((None, bs, (CONV_W - 1) * bw), lambda i: (layer, 0, 0)),
                  pl.BlockSpec((None, CONV_W, bw), lambda i: (layer, 0, 0))],
        out_specs=[full((bs, ls * bw)), full((bs, (CONV_W - 1) * bw))],
        out_shape=[jax.ShapeDtypeStruct((bs, ls * bw), BF16),
                   jax.ShapeDtypeStruct((bs, (CONV_W - 1) * bw), F32)],
        compiler_params=_params(("arbitrary",)),
        name="conv_sample",
    )(ch, cb, cc, state, conv_w)


def _f32_key(x):
    bits = lax.bitcast_convert_type(x, I32)
    return jnp.where(bits < 0, bits ^ jnp.int32(0x7FFFFFFF), bits)


def _topk_select(key, col, tau, jmax):
    return (key > tau) | ((key == tau) & (col <= jmax))


def _topk_threshold(get_chunk, nch, k, n_valid, rows, width, jmax_ref):
    def count(pred):
        def body(c, part):
            m = jnp.where(pred(c, get_chunk(c)), 1.0, 0.0)
            for b in range(width // LANES):
                part = part + m[:, b * LANES:(b + 1) * LANES]
            return part

        part = lax.fori_loop(0, nch, body, jnp.zeros((rows, LANES), F32))
        return jnp.sum(part, axis=1, keepdims=True)

    kf = float(k)
    take_all = n_valid <= kf
    cnt0 = count(lambda c, kb: kb >= 0)
    base0 = jnp.where(cnt0 >= kf, jnp.int32(0), jnp.int32(INT_MIN))
    done0 = jnp.where(take_all | (cnt0 == kf), 1, 0).astype(I32)

    def left(done):
        return jnp.sum(jnp.where(done > 0, 0.0, 1.0))

    def cond(st):
        it, _, _, nleft = st
        return (it < 31) & (nleft > 0.0)

    def body(st):
        it, base, done, _ = st
        for u in range(BISECT_STEPS_PER_CHECK):
            bit = 30 - it - u
            cand = base | jnp.where(bit >= 0, jnp.left_shift(jnp.int32(1), jnp.maximum(bit, 0)), 0)
            cnt = count(lambda c, kb, cand=cand: kb >= cand)
            base = jnp.where(cnt >= kf, cand, base)
            done = jnp.where(cnt == kf, 1, done)
        return it + BISECT_STEPS_PER_CHECK, base, done, left(done)

    _, base, done, nleft = lax.while_loop(cond, body, (jnp.int32(0), base0, done0, left(done0)))
    tau = jnp.where(take_all, jnp.int32(INT_MIN), base)
    jmax_ref[...] = jnp.full(jmax_ref.shape, 2 ** 31 - 1, I32)

    @pl.when(nleft > 0.0)
    def _():
        lane = lax.broadcasted_iota(I32, (rows, width), 1)
        need = kf - count(lambda c, kb: kb > tau)

        def idx_body(it, lo):
            cand = lo | jnp.left_shift(jnp.int32(1), 30 - it)
            cnt = count(lambda c, kb: (kb == tau) & (c * width + lane < cand))
            return jnp.where(cnt < need, cand, lo)

        lo = lax.fori_loop(0, 31, idx_body, jnp.zeros((rows, 1), I32))
        jmax = jnp.where(done > 0, jnp.int32(2 ** 31 - 1), lo)
        jmax_ref[...] = jnp.broadcast_to(jmax, jmax_ref.shape)

    return tau, jmax_ref[:, 0:1]


def _dsa_prompt_kernel(iq_ref, mw_ref, aq_ref, ik_ref, ak_ref, av_ref, o_ref, sc_ref, jm_ref,
                       *, tq, tk, n_sel, nh, scale):
    i = pl.program_id(0)
    nch = ((i + 1) * tq + tk - 1) // tk
    row = i * tq + lax.broadcasted_iota(I32, (tq, tk), 0)
    col0 = lax.broadcasted_iota(I32, (tq, tk), 1)
    w = mw_ref[...]

    def score_body(c, carry):
        off = pl.multiple_of(c * tk, tk)
        ikc = ik_ref[pl.ds(off, tk), :]
        acc = jnp.zeros((tq, tk), F32)
        for h in range(IDX_HEADS):
            d = _dot_nt(iq_ref[:, h * LANES:(h + 1) * LANES], ikc)
            acc = acc + w[:, h:h + 1] * jnp.maximum(d, 0.0)
        s = jnp.where(off + col0 <= row, acc + 0.0, NEG_BIG)
        sc_ref[c] = _f32_key(s)
        return carry

    lax.fori_loop(0, nch, score_body, 0)
    n_valid = (row[:, 0:1] + 1).astype(F32)
    tau, jmax = _topk_threshold(lambda c: sc_ref[c], nch, n_sel, n_valid, tq, tk, jm_ref)

    def att_body(c, carry):
        off = pl.multiple_of(c * tk, tk)
        col = off + col0
        bias = jnp.where(_topk_select(sc_ref[c], col, tau, jmax) & (col <= row), 0.0, NEG_BIG)
        new = []
        for h in range(nh):
            m, l, acc = carry[h]
            hs = slice(h * HEAD_DIM, (h + 1) * HEAD_DIM)
            s = _dot_nt(aq_ref[:, hs], ak_ref[pl.ds(off, tk), hs]) * scale + bias
            m_new = jnp.maximum(m, jnp.max(s, axis=1, keepdims=True))
            p = jnp.exp(s - m_new)
            alpha = jnp.exp(m - m_new)
            l = l * alpha + jnp.sum(p, axis=1, keepdims=True)
            acc = acc * alpha + _dot(p.astype(BF16), av_ref[pl.ds(off, tk), hs])
            new.append((m_new, l, acc))
        return tuple(new)

    init = tuple((jnp.full((tq, 1), NEG_BIG, F32), jnp.zeros((tq, 1), F32), jnp.zeros((tq, HEAD_DIM), F32))
                 for _ in range(nh))
    fin = lax.fori_loop(0, nch, att_body, init)
    o_ref[...] = jnp.concatenate([acc / l for _, l, acc in fin], axis=1).astype(o_ref.dtype)


def _dsa_prompt(iq_r, misc2, aq_r, ik_b, ak_b, av_b, tp, lay):
    bw = lay.bw
    tq = 128
    tk = _pick(tp, (512, 256, 128))
    n_sel = min(TOPK_MAX, tp // 4)
    kern = functools.partial(_dsa_prompt_kernel, tq=tq, tk=tk, n_sel=n_sel, nh=lay.nh,
                             scale=float(HEAD_DIM ** -0.5))
    res = lambda w: pl.BlockSpec((tp, w), lambda i: (0, 0))
    return pl.pallas_call(
        kern,
        grid=(tp // tq,),
        in_specs=[pl.BlockSpec((tq, lay.iqw), lambda i: (i, 0)),
                  pl.BlockSpec((tq, LANES), lambda i: (i, 0)),
                  pl.BlockSpec((tq, bw), lambda i: (i, 0)),
                  res(LANES), res(bw), res(bw)],
        out_specs=pl.BlockSpec((tq, bw), lambda i: (i, 0)),
        out_shape=jax.ShapeDtypeStruct((tp, bw), BF16),
        scratch_shapes=[pltpu.VMEM((tp // tk, tq, tk), I32), pltpu.VMEM((tq, LANES), I32)],
        compiler_params=_params(("arbitrary",)),
        name="dsa_prompt",
    )(iq_r, misc2, aq_r, ik_b, ak_b, av_b)


def _new_key_valid(shape, ls):
    t = lax.broadcasted_iota(I32, shape, 0) & (SAMPLE_ROWS - 1)
    c = lax.broadcasted_iota(I32, shape, 1)
    return (c <= t) & (c < ls)


def _page_spec(shape, layer, g, gi):
    return pl.BlockSpec((None, None) + shape, lambda b, j, pt: (layer, pt[b, j * g + gi], 0, 0))


def _dsa_sample_score_kernel(pt_ref, iq_ref, iw_ref, *rest, npg, g, page, ls, n_sel):
    pool_refs = rest[:g]
    new_ref, key_ref, keyn_ref, tau_ref, sc_ref, jm_ref = rest[g:]
    j = pl.program_id(1)

    def page_keys(keys, valid):
        d = _dot_nt(iq_ref[:, 0:IDX_DIM], keys.astype(BF16))
        r = jnp.maximum(d, 0.0) * iw_ref[:, 0:1]
        s = r[0:SAMPLE_ROWS, :]
        for h in range(1, IDX_HEADS):
            s = s + r[h * SAMPLE_ROWS:(h + 1) * SAMPLE_ROWS, :]
        s = s + 0.0
        return _f32_key(s if valid is None else jnp.where(valid, s, NEG_BIG))

    for gi in range(g):
        key = page_keys(pool_refs[gi][...], None)
        key_ref[:, gi * page:(gi + 1) * page] = key
        sc_ref[j * g + gi] = key

    @pl.when(j == npg // g - 1)
    def _():
        keyn = page_keys(new_ref[...], _new_key_valid((SAMPLE_ROWS, page), ls))
        keyn_ref[...] = keyn
        sc_ref[npg] = keyn
        t = lax.broadcasted_iota(I32, (SAMPLE_ROWS, 1), 0)
        n_valid = (npg * page + jnp.minimum(t + 1, ls)).astype(F32)
        tau, jmax = _topk_threshold(lambda c: sc_ref[c], npg + 1, n_sel, n_valid, SAMPLE_ROWS, page, jm_ref)
        lane = lax.broadcasted_iota(I32, tau_ref.shape, 1)
        tau_ref[...] = jnp.where(lane == 0, tau, jmax)


def _dsa_sample_score(page_table, iq_s, iw_s, pool, ik_new, layer, ls):
    bs, npg = page_table.shape
    page = pool.shape[2]
    g = _pick(npg, (8, 4, 2, 1))
    n_sel = min(TOPK_MAX, (npg * page + ls) // 4)
    kern = functools.partial(_dsa_sample_score_kernel, npg=npg, g=g, page=page, ls=ls, n_sel=n_sel)
    per_seq = lambda shape: pl.BlockSpec((None,) + shape, lambda b, j, pt: (b, 0, 0))
    grid_spec = pltpu.PrefetchScalarGridSpec(
        num_scalar_prefetch=1,
        grid=(bs, npg // g),
        in_specs=[per_seq((IDX_HEADS * SAMPLE_ROWS, LANES)), per_seq((IDX_HEADS * SAMPLE_ROWS, LANES))]
        + [_page_spec((page, IDX_DIM), layer, g, gi) for gi in range(g)]
        + [per_seq((page, IDX_DIM))],
        out_specs=[pl.BlockSpec((None, SAMPLE_ROWS, g * page), lambda b, j, pt: (b, 0, j)),
                   per_seq((SAMPLE_ROWS, page)), per_seq((SAMPLE_ROWS, LANES))],
        scratch_shapes=[pltpu.VMEM((npg + 1, SAMPLE_ROWS, page), I32), pltpu.VMEM((SAMPLE_ROWS, LANES), I32)],
    )
    return pl.pallas_call(
        kern,
        grid_spec=grid_spec,
        out_shape=[jax.ShapeDtypeStruct((bs, SAMPLE_ROWS, npg * page), I32),
                   jax.ShapeDtypeStruct((bs, SAMPLE_ROWS, page), I32),
                   jax.ShapeDtypeStruct((bs, SAMPLE_ROWS, LANES), I32)],
        compiler_params=_params(("parallel", "arbitrary")),
        name="dsa_sample_score",
    )(page_table, iq_s, iw_s, *([pool] * g), ik_new)


def _stack_heads(q, nh):
    return jnp.concatenate([q[:, h * HEAD_DIM:(h + 1) * HEAD_DIM] for h in range(nh)], axis=0)


def _unstack_heads(o, nh):
    return jnp.concatenate([o[h * SAMPLE_ROWS:(h + 1) * SAMPLE_ROWS, :] for h in range(nh)], axis=1)


def _page_patterns(page, nh):
    shift = nh.bit_length() - 1
    assert 1 << shift == nh
    shape = (nh * SAMPLE_ROWS, page * nh)
    r = lax.broadcasted_iota(I32, shape, 0)
    c = lax.broadcasted_iota(I32, shape, 1)
    own_head = (c & (nh - 1)) == (r >> 3)
    new_valid = lambda ls: ((c >> shift) <= (r & (SAMPLE_ROWS - 1))) & ((c >> shift) < ls)
    er = lax.broadcasted_iota(I32, (page, page * nh), 0)
    ec = lax.broadcasted_iota(I32, (page, page * nh), 1)
    expand = jnp.where((ec >> shift) == er, 1.0, 0.0)
    return own_head, new_valid, expand


def _online_softmax_step(logits, masks, values, m_ref, l_ref, acc_ref):
    m = m_ref[...]
    m_new = m
    for s in logits:
        m_new = jnp.maximum(m_new, jnp.max(s, axis=1, keepdims=True))
    alpha = jnp.exp(m - m_new)
    l = l_ref[...] * alpha
    acc = acc_ref[...] * alpha
    for s, msk, v in zip(logits, masks, values):
        p = jnp.where(msk, jnp.exp(s - m_new), 0.0)
        l = l + jnp.sum(p, axis=1, keepdims=True)
        acc = acc + _dot(p.astype(BF16), v)
    l_ref[...] = l
    acc_ref[...] = acc
    m_ref[...] = m_new


def _sample_attn_init(q_ref, q2_ref, m_ref, l_ref, acc_ref, nh):
    q2_ref[...] = _stack_heads(q_ref[...], nh).astype(BF16)
    m_ref[...] = jnp.full_like(m_ref, NEG_BIG)
    l_ref[...] = jnp.zeros_like(l_ref)
    acc_ref[...] = jnp.zeros_like(acc_ref)


def _sample_attn_scratch(nh):
    rows = nh * SAMPLE_ROWS
    return [pltpu.VMEM((rows, HEAD_DIM), BF16), pltpu.VMEM((rows, 1), F32),
            pltpu.VMEM((rows, 1), F32), pltpu.VMEM((rows, HEAD_DIM), F32)]


def _dsa_sample_attn_kernel(pt_ref, q_ref, key_ref, keyn_ref, tau_ref, *rest, npg, g, page, ls, nh, scale):
    kp_refs, vp_refs = rest[:g], rest[g:2 * g]
    kn_ref, vn_ref, o_ref, q2_ref, m_ref, l_ref, acc_ref = rest[2 * g:]
    j = pl.program_id(1)

    @pl.when(j == 0)
    def _():
        _sample_attn_init(q_ref, q2_ref, m_ref, l_ref, acc_ref, nh)

    own_head, new_valid, expand = _page_patterns(page, nh)
    expand = expand.astype(BF16)
    tau, jmax = tau_ref[:, 0:1], tau_ref[:, 1:2]
    lane = lax.broadcasted_iota(I32, (SAMPLE_ROWS, page), 1)

    def attend(kps, vps, key8s, col0s, valid):
        sel8 = jnp.concatenate([jnp.where(_topk_select(k8, c0 + lane, tau, jmax), 1.0, 0.0)
                                for k8, c0 in zip(key8s, col0s)], axis=0).astype(BF16)
        sel_x = _dot(sel8, expand) > 0.5
        q2 = q2_ref[...]
        logits, masks = [], []
        for n, kp in enumerate(kps):
            sel = jnp.concatenate([sel_x[n * SAMPLE_ROWS:(n + 1) * SAMPLE_ROWS, :]] * nh, axis=0)
            msk = sel & own_head if valid is None else sel & own_head & valid
            logits.append(jnp.where(msk, _dot_nt(q2, kp.astype(BF16)) * scale, NEG_BIG))
            masks.append(msk)
        _online_softmax_step(logits, masks, [vp.astype(BF16) for vp in vps], m_ref, l_ref, acc_ref)

    attend([r[...] for r in kp_refs], [r[...] for r in vp_refs],
           [key_ref[:, gi * page:(gi + 1) * page] for gi in range(g)],
           [(j * g + gi) * page for gi in range(g)], None)

    @pl.when(j == npg // g - 1)
    def _():
        attend([kn_ref[...]], [vn_ref[...]], [keyn_ref[...]], [npg * page], new_valid(ls))
        o_ref[...] = _unstack_heads(acc_ref[...] / l_ref[...], nh).astype(o_ref.dtype)


def _dsa_sample_attn(page_table, q_s, keys, keys_new, tau, kpool, vpool, k_new, v_new, layer, ls, lay):
    bs, npg = page_table.shape
    rows = kpool.shape[2]
    bw, nh = lay.bw, lay.nh
    page = rows // nh
    g = _pick(npg, (8, 4, 2, 1))
    kern = functools.partial(_dsa_sample_attn_kernel, npg=npg, g=g, page=page, ls=ls, nh=nh,
                             scale=float(HEAD_DIM ** -0.5))
    per_seq = lambda shape: pl.BlockSpec((None,) + shape, lambda b, j, pt: (b, 0, 0))
    pages = [_page_spec((rows, HEAD_DIM), layer, g, gi) for gi in range(g)]
    grid_spec = pltpu.PrefetchScalarGridSpec(
        num_scalar_prefetch=1,
        grid=(bs, npg // g),
        in_specs=[per_seq((SAMPLE_ROWS, bw)),
                  pl.BlockSpec((None, SAMPLE_ROWS, g * page), lambda b, j, pt: (b, 0, j)),
                  per_seq((SAMPLE_ROWS, page)), per_seq((SAMPLE_ROWS, LANES))]
        + pages + pages + [per_seq((rows, HEAD_DIM)), per_seq((rows, HEAD_DIM))],
        out_specs=per_seq((SAMPLE_ROWS, bw)),
        scratch_shapes=_sample_attn_scratch(nh),
    )
    return pl.pallas_call(
        kern,
        grid_spec=grid_spec,
        out_shape=jax.ShapeDtypeStruct((bs, SAMPLE_ROWS, bw), BF16),
        compiler_params=_params(("parallel", "arbitrary")),
        name="dsa_sample_attn",
    )(page_table, q_s, keys, keys_new, tau, *([kpool] * g), *([vpool] * g), k_new, v_new)


def _upper_ones(n):
    r = lax.broadcasted_iota(I32, (n, n), 0)
    c = lax.broadcasted_iota(I32, (n, n), 1)
    return jnp.where(r <= c, 1.0, 0.0).astype(F32)


def _fox_cumsum_kernel(m2_ref, crow_ref, ccol_ref, carry_ref):
    @pl.when(pl.program_id(0) == 0)
    def _():
        carry_ref[...] = jnp.zeros_like(carry_ref)

    lf_t = m2_ref[...].T
    tc = lf_t.shape[1]
    cs = _dot(lf_t, _upper_ones(tc), HIGHEST) + carry_ref[:, 0:1]
    crow_ref[...] = cs
    ccol_ref[...] = cs.T
    carry_ref[...] = jnp.broadcast_to(cs[:, tc - 1:tc], carry_ref.shape)


def _fox_cumsum(misc2, tp):
    tc = _pick(tp, (512, 256, 128))
    return pl.pallas_call(
        _fox_cumsum_kernel,
        grid=(tp // tc,),
        in_specs=[pl.BlockSpec((tc, LANES), lambda i: (i, 0))],
        out_specs=[pl.BlockSpec((LANES, tc), lambda i: (0, i)), pl.BlockSpec((tc, LANES), lambda i: (i, 0))],
        out_shape=[jax.ShapeDtypeStruct((LANES, tp), F32), jax.ShapeDtypeStruct((tp, LANES), F32)],
        scratch_shapes=[pltpu.VMEM((LANES, LANES), F32)],
        compiler_params=_params(("arbitrary",)),
        name="fox_cumsum",
    )(misc2)


def _fox_prompt_kernel(q_ref, k_ref, v_ref, ccol_ref, crow_ref, o_ref, m_ref, l_ref, acc_ref,
                       *, tq, tk, nh, ff_lane, scale):
    i, j = pl.program_id(0), pl.program_id(1)

    @pl.when(j == 0)
    def _():
        m_ref[...] = jnp.full_like(m_ref, NEG_BIG)
        l_ref[...] = jnp.zeros_like(l_ref)
        acc_ref[...] = jnp.zeros_like(acc_ref)

    def block(diagonal):
        def sub(r, carry):
            rows = pl.ds(pl.multiple_of(r * sub_q, sub_q), sub_q)
            if diagonal:
                below = lax.broadcasted_iota(I32, (sub_q, tk), 1) <= r * sub_q + lax.broadcasted_iota(I32, (sub_q, tk), 0)
            m_all, l_all, acc_all, cq = m_ref[rows, :], l_ref[rows, :], acc_ref[rows, :], ccol_ref[rows, :]
            lane = lax.broadcasted_iota(I32, m_all.shape, 1)
            accs = []
            for h in range(nh):
                hs = slice(h * HEAD_DIM, (h + 1) * HEAD_DIM)
                s = _dot_nt(q_ref[rows, hs], k_ref[:, hs]) * scale
                s = s + cq[:, ff_lane + h:ff_lane + h + 1] - crow_ref[ff_lane + h:ff_lane + h + 1, :]
                if diagonal:
                    s = jnp.where(below, s, NEG_BIG)
                m = m_all[:, h:h + 1]
                m_new = jnp.maximum(m, jnp.max(s, axis=1, keepdims=True))
                p = jnp.exp(s - m_new)
                alpha = jnp.exp(m - m_new)
                l_new = l_all[:, h:h + 1] * alpha + jnp.sum(p, axis=1, keepdims=True)
                accs.append(acc_all[:, hs] * alpha + _dot(p.astype(BF16), v_ref[:, hs]))
                m_out = jnp.where(lane == h, m_new, m_all if h == 0 else m_out)
                l_out = jnp.where(lane == h, l_new, l_all if h == 0 else l_out)
            m_ref[rows, :] = m_out
            l_ref[rows, :] = l_out
            acc_ref[rows, :] = jnp.concatenate(accs, axis=1)
            return carry

        lax.fori_loop(0, tq // sub_q, sub, 0)

    sub_q = min(tq, 128)

    @pl.when(j < i)
    def _():
        block(False)

    @pl.when(j == i)
    def _():
        block(True)
        outs = [acc_ref[:, h * HEAD_DIM:(h + 1) * HEAD_DIM] / l_ref[:, h:h + 1] for h in range(nh)]
        o_ref[...] = jnp.concatenate(outs, axis=1).astype(o_ref.dtype)


def _fox_prompt(fq_b, fk_b, fv_b, ccol, crow, tp, lay):
    bw = lay.bw
    tq = tk = _pick(tp, (512, 256, 128))
    kern = functools.partial(_fox_prompt_kernel, tq=tq, tk=tk, nh=lay.nh, ff_lane=lay.ff_lane,
                             scale=float(HEAD_DIM ** -0.5))
    return pl.pallas_call(
        kern,
        grid=(tp // tq, tp // tk),
        in_specs=[pl.BlockSpec((tq, bw), lambda i, j: (i, 0)),
                  pl.BlockSpec((tk, bw), lambda i, j: (jnp.minimum(j, i), 0)),
                  pl.BlockSpec((tk, bw), lambda i, j: (jnp.minimum(j, i), 0)),
                  pl.BlockSpec((tq, LANES), lambda i, j: (i, 0)),
                  pl.BlockSpec((LANES, tk), lambda i, j: (0, jnp.minimum(j, i)))],
        out_specs=pl.BlockSpec((tq, bw), lambda i, j: (i, 0)),
        out_shape=jax.ShapeDtypeStruct((tp, bw), BF16),
        scratch_shapes=[pltpu.VMEM((tq, LANES), F32), pltpu.VMEM((tq, LANES), F32), pltpu.VMEM((tq, bw), F32)],
        compiler_params=_params(("parallel", "arbitrary")),
        name="fox_prompt",
    )(fq_b, fk_b, fv_b, ccol, crow)


def _fox_sample_kernel(pt_ref, q_ref, *rest, npg, g, page, ls, nh, scale):
    kp_refs, vp_refs, lf_refs = rest[:g], rest[g:2 * g], rest[2 * g:3 * g]
    kn_ref, vn_ref, lfn_ref, o_ref, carry_ref, q2_ref, m_ref, l_ref, acc_ref = rest[3 * g:]
    j = pl.program_id(1)

    @pl.when(j == 0)
    def _():
        _sample_attn_init(q_ref, q2_ref, m_ref, l_ref, acc_ref, nh)
        carry_ref[...] = jnp.zeros_like(carry_ref)

    own_head, new_valid, expand = _page_patterns(page, nh)
    upper = _upper_ones(page)

    def attend(kps, vps, lfs, valid):
        within = _dot(jnp.concatenate(lfs, axis=0), upper, HIGHEST)
        carry = carry_ref[:, 0:1]
        cs = []
        for n in range(len(lfs)):
            w = within[n * SAMPLE_ROWS:(n + 1) * SAMPLE_ROWS, :]
            cs.append(w + carry)
            carry = carry + w[:, page - 1:page]
        carry_ref[...] = jnp.broadcast_to(carry, carry_ref.shape)
        cs_x = _dot(jnp.concatenate(cs, axis=0), expand, HIGHEST)
        q2 = q2_ref[...]
        msk = own_head if valid is None else own_head & valid
        logits = []
        for n, kp in enumerate(kps):
            bias = jnp.concatenate([jnp.broadcast_to(cs_x[n * SAMPLE_ROWS + h:n * SAMPLE_ROWS + h + 1, :],
                                                     (SAMPLE_ROWS, page * nh)) for h in range(nh)], axis=0)
            logits.append(jnp.where(msk, _dot_nt(q2, kp.astype(BF16)) * scale - bias, NEG_BIG))
        _online_softmax_step(logits, [msk] * len(kps), [vp.astype(BF16) for vp in vps], m_ref, l_ref, acc_ref)

    attend([r[...] for r in kp_refs], [r[...] for r in vp_refs], [r[...] for r in lf_refs], None)

    @pl.when(j == npg // g - 1)
    def _():
        attend([kn_ref[...]], [vn_ref[...]], [lfn_ref[...]], new_valid(ls))
        o_ref[...] = _unstack_heads(acc_ref[...] / l_ref[...], nh).astype(o_ref.dtype)


def _fox_sample(page_table, q_s, kpool, vpool, lfpool_t, k_new, v_new, lf_new, layer, ls, lay):
    bs, npg = page_table.shape
    rows = kpool.shape[2]
    bw, nh = lay.bw, lay.nh
    page = rows // nh
    g = _pick(npg, (8, 4, 2, 1))
    kern = functools.partial(_fox_sample_kernel, npg=npg, g=g, page=page, ls=ls, nh=nh,
                             scale=float(HEAD_DIM ** -0.5))
    per_seq = lambda shape: pl.BlockSpec((None,) + shape, lambda b, j, pt: (b, 0, 0))
    pages = [_page_spec((rows, HEAD_DIM), layer, g, gi) for gi in range(g)]
    grid_spec = pltpu.PrefetchScalarGridSpec(
        num_scalar_prefetch=1,
        grid=(bs, npg // g),
        in_specs=[per_seq((SAMPLE_ROWS, bw))] + pages + pages
        + [_page_spec((SAMPLE_ROWS, page), layer, g, gi) for gi in range(g)]
        + [per_seq((rows, HEAD_DIM)), per_seq((rows, HEAD_DIM)), per_seq((SAMPLE_ROWS, page))],
        out_specs=per_seq((SAMPLE_ROWS, bw)),
        scratch_shapes=[pltpu.VMEM((SAMPLE_ROWS, LANES), F32)] + _sample_attn_scratch(nh),
    )
    return pl.pallas_call(
        kern,
        grid_spec=grid_spec,
        out_shape=jax.ShapeDtypeStruct((bs, SAMPLE_ROWS, bw), BF16),
        compiler_params=_params(("parallel", "arbitrary")),
        name="fox_sample",
    )(page_table, q_s, *([kpool] * g), *([vpool] * g), *([lfpool_t] * g), k_new, v_new, lf_new)


def _hgrn_lower_bound(lg_ref, layer):
    lg = lg_ref[...]
    e = jnp.exp(lg - jnp.max(lg, axis=0, keepdims=True))
    p = e / jnp.sum(e, axis=0, keepdims=True)
    lb = jnp.zeros((1, lg.shape[1]), F32)
    for i in range(1, layer + 1):
        lb = lb + p[i:i + 1, :]
    return lb


def _hgrn_rows(hq, hf, hi, lb, chunk, row_valid, nh):
    r = hq.shape[0]
    pos = lax.broadcasted_iota(I32, hq.shape, 0) & (chunk - 1)
    lf = jnp.log(lb + (1.0 - lb) * jax.nn.sigmoid(hf))
    kk = (1.0 - lb) * jax.nn.sigmoid(-hf)
    if row_valid is not None:
        lf = jnp.where(row_valid, lf, 0.0)
        kk = jnp.where(row_valid, kk, 0.0)
    b = lf
    step = 1
    while step < chunk:
        b = b + jnp.where(pos >= step, pltpu.roll(b, step, 0), 0.0)
        step *= 2
    suf = lf
    step = 1
    while step < chunk:
        suf = suf + jnp.where(pos + step < chunk, pltpu.roll(suf, r - step, 0), 0.0)
        step *= 2
    tail = suf - lf
    qe = hq * jnp.exp(b)
    kd = kk * jnp.exp(tail)
    etot = jnp.exp(b + tail)
    o_intra = jnp.zeros_like(hq)
    for d in range(chunk):
        if d == 0:
            kk_d, b_d, i_d = kk, b, hi
        else:
            kk_d, b_d, i_d = pltpu.roll(kk, d, 0), pltpu.roll(b, d, 0), pltpu.roll(hi, d, 0)
        dec = jnp.exp(jnp.where(pos >= d, b - b_d, NEG_BIG))
        prod = hq * dec * kk_d
        a = jnp.concatenate(
            [jnp.broadcast_to(jnp.sum(prod[:, h * HEAD_DIM:(h + 1) * HEAD_DIM], axis=1, keepdims=True), (r, HEAD_DIM))
             for h in range(nh)], axis=1)
        o_intra = o_intra + a * i_d
    return qe, kd, etot, o_intra


def _hgrn_finish(o, hg, ng, nh):
    outs = []
    for h in range(nh):
        oh = o[:, h * HEAD_DIM:(h + 1) * HEAD_DIM]
        outs.append(oh * lax.rsqrt(jnp.mean(oh * oh, axis=1, keepdims=True) + EPS))
    return jnp.concatenate(outs, axis=1) * ng * (hg * jax.nn.sigmoid(hg))


def _hgrn_prompt_kernel(hq_ref, hf_ref, hi_ref, hg_ref, lg_ref, ng_ref, y_ref, s_ref,
                        st_ref, qe_ref, kd_ref, et_ref, oi_ref, *, layer, chunk, nh):
    @pl.when(pl.program_id(0) == 0)
    def _():
        st_ref[...] = jnp.zeros_like(st_ref)

    lb = _hgrn_lower_bound(lg_ref, layer)
    hi = hi_ref[...]
    qe, kd, etot, o_intra = _hgrn_rows(hq_ref[...], hf_ref[...], hi, lb, chunk, None, nh)
    qe_ref[...] = qe.astype(BF16)
    kd_ref[...] = kd.astype(BF16)
    et_ref[...] = etot
    tm = hi.shape[0]

    def chunk_body(c, carry):
        rows = pl.ds(pl.multiple_of(c * chunk, chunk), chunk)
        for h in range(nh):
            hs = slice(h * HEAD_DIM, (h + 1) * HEAD_DIM)
            st = st_ref[h]
            oi_ref[rows, hs] = _dot_nt(qe_ref[rows, hs], st.astype(BF16))
            e = et_ref[rows, hs]
            st_ref[h] = st * e[0:1, :] + _dot_tn(hi_ref[rows, hs].astype(BF16), kd_ref[rows, hs])
        return carry

    lax.fori_loop(0, tm // chunk, chunk_body, 0)
    y_ref[...] = _hgrn_finish(oi_ref[...] + o_intra, hg_ref[...], ng_ref[...], nh).astype(y_ref.dtype)

    @pl.when(pl.program_id(0) == pl.num_programs(0) - 1)
    def _():
        for h in range(nh):
            s_ref[h] = st_ref[h].T


def _hgrn_prompt(z, lb_logits, norm_g, tp, lay, layer):
    bw, nh = lay.bw, lay.nh
    tm = _pick(tp, (512, 256, 128))
    dep = lb_logits.shape[0]
    col = lambda off: pl.BlockSpec((tm, bw), lambda i: (i, off // bw))
    kern = functools.partial(_hgrn_prompt_kernel, layer=layer, chunk=HG_CHUNK_PROMPT, nh=nh)
    return pl.pallas_call(
        kern,
        grid=(tp // tm,),
        in_specs=[col(lay.hq), col(lay.hf), col(lay.hi), col(lay.hg),
                  pl.BlockSpec((dep, bw), lambda i: (0, 0)),
                  pl.BlockSpec((None, 1, bw), lambda i: (layer, 0, 0))],
        out_specs=[pl.BlockSpec((tm, bw), lambda i: (i, 0)),
                   pl.BlockSpec((nh, HEAD_DIM, HEAD_DIM), lambda i: (0, 0, 0))],
        out_shape=[jax.ShapeDtypeStruct((tp, bw), BF16), jax.ShapeDtypeStruct((nh, HEAD_DIM, HEAD_DIM), F32)],
        scratch_shapes=[pltpu.VMEM((nh, HEAD_DIM, HEAD_DIM), F32), pltpu.VMEM((tm, bw), BF16),
                        pltpu.VMEM((tm, bw), BF16), pltpu.VMEM((tm, bw), F32), pltpu.VMEM((tm, bw), F32)],
        compiler_params=_params(("arbitrary",)),
        name="hgrn_prompt",
    )(z, z, z, z, lb_logits, norm_g)


def _hgrn_sample_kernel(hq_ref, hf_ref, hi_ref, hg_ref, lg_ref, ng_ref, s0_ref, y_ref, s_ref,
                        *, layer, ls, nh):
    lb = _hgrn_lower_bound(lg_ref, layer)
    hi = hi_ref[...]
    row_valid = lax.broadcasted_iota(I32, hi.shape, 0) < ls
    hi = jnp.where(row_valid, hi, 0.0)
    qe, kd, etot, o_intra = _hgrn_rows(hq_ref[...], hf_ref[...], hi, lb, SAMPLE_ROWS, row_valid, nh)
    outs = []
    for h in range(nh):
        hs = slice(h * HEAD_DIM, (h + 1) * HEAD_DIM)
        st = s0_ref[h].T
        outs.append(_dot_nt(qe[:, hs].astype(BF16), st.astype(BF16)))
        st = st * etot[0:1, hs] + _dot_tn(hi[:, hs].astype(BF16), kd[:, hs].astype(BF16))
        s_ref[h] = st.T
    o = jnp.concatenate(outs, axis=1) + o_intra
    y_ref[...] = _hgrn_finish(o, hg_ref[...], ng_ref[...], nh).astype(y_ref.dtype)


def _hgrn_sample(hq, hf, hi, hg, lb_logits, norm_g, state, layer, ls, lay):
    bs = hq.shape[0]
    bw, nh = lay.bw, lay.nh
    dep = lb_logits.shape[0]
    rows = pl.BlockSpec((None, SAMPLE_ROWS, bw), lambda b: (b, 0, 0))
    kern = functools.partial(_hgrn_sample_kernel, layer=layer, ls=ls, nh=nh)
    return pl.pallas_call(
        kern,
        grid=(bs,),
        in_specs=[rows, rows, rows, rows,
                  pl.BlockSpec((dep, bw), lambda b: (0, 0)),
                  pl.BlockSpec((None, 1, bw), lambda b: (layer, 0, 0)),
                  pl.BlockSpec((None, None, nh, HEAD_DIM, HEAD_DIM), lambda b: (layer, b, 0, 0, 0))],
        out_specs=[rows, pl.BlockSpec((None, nh, HEAD_DIM, HEAD_DIM), lambda b: (b, 0, 0, 0))],
        out_shape=[jax.ShapeDtypeStruct((bs, SAMPLE_ROWS, bw), BF16),
                   jax.ShapeDtypeStruct((bs, nh, HEAD_DIM, HEAD_DIM), F32)],
        compiler_params=_params(("parallel",)),
        name="hgrn_sample",
    )(hq, hf, hi, hg, lb_logits, norm_g, state)


def _pad_rows(a, bs, ls, rows):
    return jnp.pad(a.reshape(bs, ls, a.shape[-1]), ((0, 0), (0, rows - ls), (0, 0)))


def kernel(x_prompt, x_sample, cache_dsa_k, cache_dsa_v, cache_dsa_kidx, cache_fox_k, cache_fox_v,
           cache_fox_logf, state_conv, state_hgrn, page_table, norm_mix_g, w_in, conv_w, fox_fb,
           hgrn_lb_logits, hgrn_norm_g, w_branch, w_out, norm_ffn_g, w_up, w_down, final_norm_g):
    bp, tp, d = x_prompt.shape
    bs, ls, _ = x_sample.shape
    assert bp == 1 and ls <= SAMPLE_ROWS
    depth = w_in.shape[0]
    lay = _Layout(d)
    bw, nh = lay.bw, lay.nh
    ts = bs * ls
    npg = page_table.shape[1]
    n_pool, page = cache_dsa_k.shape[1], cache_dsa_k.shape[2]
    assert page == LANES and tp % LANES == 0
    past = npg * page

    pos = jnp.concatenate([jnp.arange(tp), past + jnp.tile(jnp.arange(ls), bs)])
    tab_a, half_a = _rope_tables(pos, HEAD_DIM)
    tab_i, half_i = _rope_tables(pos, IDX_DIM)

    w_in_r = _relayout_w_in(w_in, lay)
    w_br_b, w_out_b = w_branch.astype(BF16), w_out.astype(BF16)
    w_up_b, w_down_b = w_up.astype(BF16), w_down.astype(BF16)
    fb_rows = jnp.pad(fox_fb, ((0, 0), (lay.ff_lane, LANES - lay.ff_lane - nh)))[:, None, :]
    g_mix, g_ffn, g_hg = norm_mix_g[:, None, :], norm_ffn_g[:, None, :], hgrn_norm_g[:, None, :]
    as_pages = lambda c: c.reshape(depth, n_pool, page * nh, HEAD_DIM)
    kpool_a, vpool_a, kpool_f, vpool_f = map(as_pages, (cache_dsa_k, cache_dsa_v, cache_fox_k, cache_fox_v))
    new_page = lambda a: jnp.pad(a.reshape(bs, ls * nh, HEAD_DIM), ((0, 0), (0, (page - ls) * nh), (0, 0)))
    lfpool_t = jnp.pad(jnp.swapaxes(cache_fox_logf, 2, 3), ((0, 0), (0, 0), (0, SAMPLE_ROWS - nh), (0, 0)))
    conv_state = state_conv.reshape(depth, bs, (CONV_W - 1) * bw)

    x = jnp.concatenate([x_prompt[0], x_sample.reshape(ts, d)], axis=0)
    new_p, new_s = [], []
    for l in range(depth):
        z = _in_proj(x, g_mix, w_in_r, l)
        (aq_r, ak_r, ak_b, av_b, iq_r, ik_r, ik_b, misc2,
         fq_b, fk_b, fv_b) = _post(z, tab_a, tab_i, fb_rows, lay, l, half_a, half_i)
        zs = z[tp:]
        sec = lambda off: zs[:, off:off + bw]

        y_conv_p, u_last = _conv_prompt(z, conv_w, tp, lay, l)
        flat = lambda a: a.reshape(bs, ls * bw)
        y_conv_s, conv_new_s = _conv_sample(flat(sec(lay.ch)), flat(sec(lay.cb)), flat(sec(lay.cc)),
                                            conv_state, conv_w, l, bs, ls, bw)

        y_dsa_p = _dsa_prompt(iq_r, misc2, aq_r, ik_b, ak_b, av_b, tp, lay)
        iq_s = _pad_rows(iq_r[tp:], bs, ls, SAMPLE_ROWS).reshape(bs, SAMPLE_ROWS, IDX_HEADS, LANES)
        iq_s = jnp.swapaxes(iq_s, 1, 2).reshape(bs, IDX_HEADS * SAMPLE_ROWS, LANES)
        iw_s = jnp.swapaxes(_pad_rows(misc2[tp:, :IDX_HEADS], bs, ls, SAMPLE_ROWS), 1, 2)
        iw_s = jnp.broadcast_to(iw_s.reshape(bs, IDX_HEADS * SAMPLE_ROWS, 1), (bs, IDX_HEADS * SAMPLE_ROWS, LANES))
        ik_new = _pad_rows(ik_r[tp:, :IDX_DIM], bs, ls, page)
        keys, keys_new, tau = _dsa_sample_score(page_table, iq_s, iw_s, cache_dsa_kidx, ik_new, l, ls)
        y_dsa_s = _dsa_sample_attn(page_table, _pad_rows(aq_r[tp:], bs, ls, SAMPLE_ROWS), keys, keys_new, tau,
                                   kpool_a, vpool_a, new_page(ak_b[tp:]), new_page(av_b[tp:]), l, ls, lay)

        crow, ccol = _fox_cumsum(misc2, tp)
        y_fox_p = _fox_prompt(fq_b, fk_b, fv_b, ccol, crow, tp, lay)
        logf_s = misc2[tp:, lay.ff_lane:lay.ff_lane + nh]
        lf_new = jnp.pad(jnp.swapaxes(logf_s.reshape(bs, ls, nh), 1, 2),
                         ((0, 0), (0, SAMPLE_ROWS - nh), (0, page - ls)))
        y_fox_s = _fox_sample(page_table, _pad_rows(fq_b[tp:], bs, ls, SAMPLE_ROWS), kpool_f, vpool_f, lfpool_t,
                              new_page(fk_b[tp:]), new_page(fv_b[tp:]), lf_new, l, ls, lay)

        y_hg_p, s_hg_p = _hgrn_prompt(z, hgrn_lb_logits, g_hg, tp, lay, l)
        pad8 = lambda off: _pad_rows(sec(off), bs, ls, SAMPLE_ROWS)
        y_hg_s, s_hg_s = _hgrn_sample(pad8(lay.hq), pad8(lay.hf), pad8(lay.hi), pad8(lay.hg),
                                      hgrn_lb_logits, g_hg, state_hgrn, l, ls, lay)

        unpad = lambda a: a[:, :ls].reshape(ts, bw)
        branches = [jnp.concatenate([y_conv_p, y_conv_s.reshape(ts, bw)], axis=0),
                    jnp.concatenate([y_dsa_p, unpad(y_dsa_s)], axis=0),
                    jnp.concatenate([y_fox_p, unpad(y_fox_s)], axis=0),
                    jnp.concatenate([y_hg_p, unpad(y_hg_s)], axis=0)]
        x = _mix(branches, z, w_br_b, w_out_b, x, lay, l)
        x = _ffn(x, g_ffn, w_up_b, w_down_b, l)

        heads = lambda a, b_, l_: a.reshape(b_, l_, nh, HEAD_DIM)
        zp = lambda off: z[:tp, off:off + bw]
        new_p.append((heads(ak_r[:tp], 1, tp), heads(zp(lay.av), 1, tp), ik_r[:tp, :IDX_DIM].reshape(1, tp, IDX_DIM),
                      heads(zp(lay.fk), 1, tp), heads(zp(lay.fv), 1, tp),
                      misc2[:tp, lay.ff_lane:lay.ff_lane + nh].reshape(1, tp, nh),
                      u_last[8 - (CONV_W - 1):].reshape(1, CONV_W - 1, bw), s_hg_p[None]))
        new_s.append((heads(ak_r[tp:], bs, ls), heads(sec(lay.av), bs, ls), ik_r[tp:, :IDX_DIM].reshape(bs, ls, IDX_DIM),
                      heads(sec(lay.fk), bs, ls), heads(sec(lay.fv), bs, ls), logf_s.reshape(bs, ls, nh),
                      conv_new_s.reshape(bs, CONV_W - 1, bw), s_hg_s))

    y = _final_norm(x, final_norm_g[None, :])
    y_prompt = y[:tp][None]
    y_sample = y[tp:].reshape(bs, ls, d)
    outs_p = [jnp.stack(t) for t in zip(*new_p)]
    outs_s = [jnp.stack(t) for t in zip(*new_s)]
    return (y_prompt, y_sample, *outs_p, *outs_s)
```

```python
import functools

import jax
import jax.numpy as jnp
from jax import lax
from jax.experimental import pallas as pl
from jax.experimental.pallas import tpu as pltpu

F32 = jnp.float32
BF16 = jnp.bfloat16
I32 = jnp.int32

LANES = 128
HEAD_DIM = 128
IDX_DIM = 64
IDX_HEADS = 16
CONV_W = 3
TOPK_MAX = 256
ROPE_THETA = 500000.0
ROT_FRACTION = 4
EPS = 1e-6
NEG_BIG = -1e30
INT_MIN = -2 ** 31
SAMPLE_ROWS = 8
HG_CHUNK_PROMPT = 16
BISECT_STEPS_PER_CHECK = 4
VMEM_LIMIT = 56 * 1024 * 1024
HIGHEST = lax.Precision.HIGHEST


def _dot(a, b, precision=None):
    return jnp.dot(a, b, preferred_element_type=F32, precision=precision)


def _dot_nt(a, b):
    return lax.dot_general(a, b, (((1,), (1,)), ((), ())), preferred_element_type=F32)


def _dot_tn(a, b):
    return lax.dot_general(a, b, (((0,), (0,)), ((), ())), preferred_element_type=F32)


def _pick(n, cands):
    for c in cands:
        if n % c == 0:
            return c
    raise ValueError(f"no tile for {n} in {cands}")


def _params(sem, vmem=VMEM_LIMIT):
    return pltpu.CompilerParams(dimension_semantics=sem, vmem_limit_bytes=vmem)


def _log_sigmoid(x):
    return jnp.minimum(x, 0.0) - jnp.log1p(jnp.exp(-jnp.abs(x)))


def _rms(x, g):
    return x * lax.rsqrt(jnp.mean(x * x, axis=-1, keepdims=True) + EPS) * g


class _Layout:
    def __init__(self, d_model):
        bw = d_model // 4
        assert bw % HEAD_DIM == 0
        self.d = d_model
        self.bw = bw
        self.nh = bw // HEAD_DIM
        self.iqw = IDX_HEADS * LANES
        assert self.iqw % bw == 0 and IDX_HEADS + self.nh <= LANES
        o = 0
        for name in ("ch", "cb", "cc", "aq", "ak", "av", "fq", "fk"):
            setattr(self, name, o)
            o += bw
        self.iq = o
        o += self.iqw
        for name in ("fv", "hq", "hf", "hi", "hg"):
            setattr(self, name, o)
            o += bw
        self.ikp = o
        o += LANES
        self.misc = o
        o += LANES
        o = -(-o // bw) * bw
        self.gate = o
        o += 4 * d_model
        self.n = o
        self.ff_lane = IDX_HEADS


def _relayout_w_in(w_in, lay):
    bw, nh = lay.bw, lay.nh
    dep, d, _ = w_in.shape
    o = 0

    def take(width):
        nonlocal o
        s = w_in[:, :, o:o + width]
        o += width
        return s

    first6 = take(6 * bw)
    iq = take(IDX_HEADS * IDX_DIM)
    ik = take(IDX_DIM)
    iw = take(IDX_HEADS)
    fq, fk, fv = take(bw), take(bw), take(bw)
    ff = take(nh)
    h4 = take(4 * bw)
    gate = take(4 * lay.d)
    assert o == w_in.shape[-1]
    iq_pad = jnp.pad(iq.reshape(dep, d, IDX_HEADS, IDX_DIM),
                     ((0, 0), (0, 0), (0, 0), (0, LANES - IDX_DIM))).reshape(dep, d, lay.iqw)
    ik_pad = jnp.pad(ik, ((0, 0), (0, 0), (0, LANES - IDX_DIM)))
    misc = jnp.pad(jnp.concatenate([iw, ff], axis=-1), ((0, 0), (0, 0), (0, LANES - IDX_HEADS - nh)))
    pad = jnp.zeros((dep, d, lay.gate - lay.misc - LANES), w_in.dtype)
    out = jnp.concatenate([first6, fq, fk, iq_pad, fv, h4, ik_pad, misc, pad, gate], axis=-1)
    assert out.shape[-1] == lay.n
    return out.astype(BF16)


def _rope_tables(pos, head_dim):
    d_rot = head_dim // ROT_FRACTION
    half = d_rot // 2
    inv = ROPE_THETA ** (-jnp.arange(half, dtype=F32) / half)
    ang = pos.astype(F32)[:, None] * inv[None, :]
    cos, sin = jnp.cos(ang), jnp.sin(ang)
    t = pos.shape[0]
    rest = LANES - d_rot
    c = jnp.concatenate([cos, cos, jnp.ones((t, rest), F32)], axis=1)
    s1 = jnp.concatenate([-sin, jnp.zeros((t, half + rest), F32)], axis=1)
    s2 = jnp.concatenate([jnp.zeros((t, half), F32), sin, jnp.zeros((t, rest), F32)], axis=1)
    return jnp.concatenate([c, s1, s2], axis=1), half


def _in_proj_kernel(x_ref, g_ref, w_ref, z_ref, xn_ref):
    @pl.when(pl.program_id(1) == 0)
    def _():
        xn_ref[...] = _rms(x_ref[...], g_ref[...]).astype(BF16)

    z_ref[...] = _dot(xn_ref[...], w_ref[...])


def _in_proj(x, g, w_r, layer):
    t, d = x.shape
    n = w_r.shape[-1]
    tm = _pick(t, (640, 512, 256, 128))
    tn = _pick(n, (1024, 512))
    return pl.pallas_call(
        _in_proj_kernel,
        grid=(t // tm, n // tn),
        in_specs=[pl.BlockSpec((tm, d), lambda i, j: (i, 0)),
                  pl.BlockSpec((None, 1, d), lambda i, j: (layer, 0, 0)),
                  pl.BlockSpec((None, d, tn), lambda i, j: (layer, 0, j))],
        out_specs=pl.BlockSpec((tm, tn), lambda i, j: (i, j)),
        out_shape=jax.ShapeDtypeStruct((t, n), F32),
        scratch_shapes=[pltpu.VMEM((tm, d), BF16)],
        compiler_params=_params(("parallel", "arbitrary")),
        name="in_proj",
    )(x, g, w_r)


def _rope(x, tab, half, reps):
    w = x.shape[1]
    c = jnp.concatenate([tab[:, 0:LANES]] * reps, axis=1)
    s1 = jnp.concatenate([tab[:, LANES:2 * LANES]] * reps, axis=1)
    s2 = jnp.concatenate([tab[:, 2 * LANES:3 * LANES]] * reps, axis=1)
    return x * c + pltpu.roll(x, w - half, 1) * s1 + pltpu.roll(x, half, 1) * s2


def _post_kernel(aq_ref, ak_ref, av_ref, iq_ref, ikp_ref, misc_ref, fq_ref, fk_ref, fv_ref,
                 ta_ref, ti_ref, fb_ref,
                 aqr_ref, akr_ref, akb_ref, avb_ref, iqr_ref, ikr_ref, ikb_ref, m2_ref,
                 fqb_ref, fkb_ref, fvb_ref, *, nh, half_a, half_i, idx_scale):
    ta, ti = ta_ref[...], ti_ref[...]
    fqb_ref[...] = fq_ref[...].astype(BF16)
    fkb_ref[...] = fk_ref[...].astype(BF16)
    fvb_ref[...] = fv_ref[...].astype(BF16)
    aqr_ref[...] = _rope(aq_ref[...], ta, half_a, nh).astype(BF16)
    akr = _rope(ak_ref[...], ta, half_a, nh)
    akr_ref[...] = akr
    akb_ref[...] = akr.astype(BF16)
    avb_ref[...] = av_ref[...].astype(BF16)
    iqr_ref[...] = _rope(iq_ref[...], ti, half_i, IDX_HEADS).astype(BF16)
    ikr = _rope(ikp_ref[...], ti, half_i, 1)
    ikr_ref[...] = ikr
    ikb_ref[...] = ikr.astype(BF16)
    m = misc_ref[...]
    lane = lax.broadcasted_iota(I32, m.shape, 1)
    logf = _log_sigmoid(m + fb_ref[...])
    m2_ref[...] = jnp.where(lane < IDX_HEADS, m * idx_scale,
                            jnp.where(lane < IDX_HEADS + nh, logf, 0.0))


def _post(z, tab_a, tab_i, fb_row, lay, layer, half_a, half_i):
    t = z.shape[0]
    bw = lay.bw
    tm = _pick(t, (320, 256, 128, 64, 32, 16))
    col = lambda off, w: pl.BlockSpec((tm, w), lambda i: (i, off // w))
    row = lambda w: pl.BlockSpec((tm, w), lambda i: (i, 0))
    kern = functools.partial(_post_kernel, nh=lay.nh, half_a=half_a, half_i=half_i,
                             idx_scale=float((IDX_HEADS * IDX_DIM) ** -0.5))
    return pl.pallas_call(
        kern,
        grid=(t // tm,),
        in_specs=[col(lay.aq, bw), col(lay.ak, bw), col(lay.av, bw), col(lay.iq, lay.iqw),
                  col(lay.ikp, LANES), col(lay.misc, LANES), col(lay.fq, bw), col(lay.fk, bw), col(lay.fv, bw),
                  row(3 * LANES), row(3 * LANES),
                  pl.BlockSpec((None, 1, LANES), lambda i: (layer, 0, 0))],
        out_specs=[row(bw), row(bw), row(bw), row(bw), row(lay.iqw), row(LANES), row(LANES), row(LANES),
                   row(bw), row(bw), row(bw)],
        out_shape=[jax.ShapeDtypeStruct((t, bw), BF16), jax.ShapeDtypeStruct((t, bw), F32),
                   jax.ShapeDtypeStruct((t, bw), BF16), jax.ShapeDtypeStruct((t, bw), BF16),
                   jax.ShapeDtypeStruct((t, lay.iqw), BF16), jax.ShapeDtypeStruct((t, LANES), F32),
                   jax.ShapeDtypeStruct((t, LANES), BF16), jax.ShapeDtypeStruct((t, LANES), F32),
                   jax.ShapeDtypeStruct((t, bw), BF16), jax.ShapeDtypeStruct((t, bw), BF16),
                   jax.ShapeDtypeStruct((t, bw), BF16)],
        compiler_params=_params(("parallel",)),
        name="post_proj",
    )(z, z, z, z, z, z, z, z, z, tab_a, tab_i, fb_row)


def _mix_kernel(b0_ref, b1_ref, b2_ref, b3_ref, g0_ref, g1_ref, g2_ref, g3_ref,
                wbr_ref, wout_ref, x_ref, o_ref, acc_ref):
    dt = pl.program_id(1)

    @pl.when(dt == 0)
    def _():
        acc_ref[...] = jnp.zeros_like(acc_ref)

    m = None
    for n, (b_ref, g_ref) in enumerate(((b0_ref, g0_ref), (b1_ref, g1_ref), (b2_ref, g2_ref), (b3_ref, g3_ref))):
        term = jax.nn.sigmoid(g_ref[...]) * _dot(b_ref[...], wbr_ref[n])
        m = term if m is None else m + term
    acc_ref[...] += _dot(m.astype(BF16), wout_ref[...])

    @pl.when(dt == pl.num_programs(1) - 1)
    def _():
        o_ref[...] = x_ref[...] + acc_ref[...]


def _mix(branches, z, w_br, w_out, x, lay, layer):
    t, d = x.shape
    bw = lay.bw
    tm = _pick(t, (640, 512, 256, 128))
    td = 512
    assert d % td == 0 and lay.gate % td == 0
    gspec = lambda n: pl.BlockSpec((tm, td), lambda i, j: (i, (lay.gate + n * d) // td + j))
    bspec = pl.BlockSpec((tm, bw), lambda i, j: (i, 0))
    return pl.pallas_call(
        _mix_kernel,
        grid=(t // tm, d // td),
        in_specs=[bspec, bspec, bspec, bspec, gspec(0), gspec(1), gspec(2), gspec(3),
                  pl.BlockSpec((None, 4, bw, td), lambda i, j: (layer, 0, 0, j)),
                  pl.BlockSpec((None, td, d), lambda i, j: (layer, j, 0)),
                  pl.BlockSpec((tm, d), lambda i, j: (i, 0))],
        out_specs=pl.BlockSpec((tm, d), lambda i, j: (i, 0)),
        out_shape=jax.ShapeDtypeStruct((t, d), F32),
        scratch_shapes=[pltpu.VMEM((tm, d), F32)],
        compiler_params=_params(("parallel", "arbitrary")),
        name="branch_mix",
    )(*branches, z, z, z, z, w_br, w_out, x)


def _ffn_kernel(x_ref, g_ref, wu_ref, wd_ref, o_ref, xn_ref, acc_ref):
    f = pl.program_id(1)

    @pl.when(f == 0)
    def _():
        xn_ref[...] = _rms(x_ref[...], g_ref[...]).astype(BF16)
        acc_ref[...] = jnp.zeros_like(acc_ref)

    h = jnp.maximum(_dot(xn_ref[...], wu_ref[...]), 0.0)
    acc_ref[...] += _dot((h * h).astype(BF16), wd_ref[...])

    @pl.when(f == pl.num_programs(1) - 1)
    def _():
        o_ref[...] = x_ref[...] + acc_ref[...]


def _ffn(x, g, w_up, w_down, layer):
    t, d = x.shape
    dff = w_up.shape[-1]
    tm = _pick(t, (640, 512, 256, 128))
    tf = _pick(dff, (512, 256, 128))
    return pl.pallas_call(
        _ffn_kernel,
        grid=(t // tm, dff // tf),
        in_specs=[pl.BlockSpec((tm, d), lambda i, j: (i, 0)),
                  pl.BlockSpec((None, 1, d), lambda i, j: (layer, 0, 0)),
                  pl.BlockSpec((None, d, tf), lambda i, j: (layer, 0, j)),
                  pl.BlockSpec((None, tf, d), lambda i, j: (layer, j, 0))],
        out_specs=pl.BlockSpec((tm, d), lambda i, j: (i, 0)),
        out_shape=jax.ShapeDtypeStruct((t, d), F32),
        scratch_shapes=[pltpu.VMEM((tm, d), BF16), pltpu.VMEM((tm, d), F32)],
        compiler_params=_params(("parallel", "arbitrary")),
        name="ffn",
    )(x, g, w_up, w_down)


def _final_norm_kernel(x_ref, g_ref, o_ref):
    o_ref[...] = _rms(x_ref[...], g_ref[...])


def _final_norm(x, g):
    t, d = x.shape
    tm = _pick(t, (640, 512, 256, 128))
    return pl.pallas_call(
        _final_norm_kernel,
        grid=(t // tm,),
        in_specs=[pl.BlockSpec((tm, d), lambda i: (i, 0)), pl.BlockSpec((1, d), lambda i: (0, 0))],
        out_specs=pl.BlockSpec((tm, d), lambda i: (i, 0)),
        out_shape=jax.ShapeDtypeStruct((t, d), F32),
        compiler_params=_params(("parallel",)),
        name="final_norm",
    )(x, g)


def _conv_prompt_kernel(ch_ref, cb_ref, cc_ref, w_ref, y_ref, last_ref, carry_ref):
    @pl.when(pl.program_id(0) == 0)
    def _():
        carry_ref[...] = jnp.zeros_like(carry_ref)

    u = cc_ref[...] * ch_ref[...]
    tm = u.shape[0]
    rid = lax.broadcasted_iota(I32, u.shape, 0)
    prev = carry_ref[...]
    acc = w_ref[CONV_W - 1:CONV_W, :] * u
    for back in range(1, CONV_W):
        head = jnp.concatenate([pltpu.roll(prev, back, 0), jnp.zeros((tm - 8, u.shape[1]), F32)], axis=0)
        shifted = jnp.where(rid < back, head, pltpu.roll(u, back, 0))
        acc = acc + w_ref[CONV_W - 1 - back:CONV_W - back, :] * shifted
    y_ref[...] = (cb_ref[...] * acc).astype(y_ref.dtype)
    carry_ref[...] = u[tm - 8:tm, :]
    last_ref[...] = u[tm - 8:tm, :]


def _conv_prompt(z, conv_w, tp, lay, layer):
    bw = lay.bw
    tm = _pick(tp, (512, 256, 128))
    col = lambda off: pl.BlockSpec((tm, bw), lambda i: (i, off // bw))
    return pl.pallas_call(
        _conv_prompt_kernel,
        grid=(tp // tm,),
        in_specs=[col(lay.ch), col(lay.cb), col(lay.cc),
                  pl.BlockSpec((None, CONV_W, bw), lambda i: (layer, 0, 0))],
        out_specs=[pl.BlockSpec((tm, bw), lambda i: (i, 0)), pl.BlockSpec((8, bw), lambda i: (0, 0))],
        out_shape=[jax.ShapeDtypeStruct((tp, bw), BF16), jax.ShapeDtypeStruct((8, bw), F32)],
        scratch_shapes=[pltpu.VMEM((8, bw), F32)],
        compiler_params=_params(("arbitrary",)),
        name="conv_prompt",
    )(z, z, z, conv_w)


def _conv_sample_kernel(ch_ref, cb_ref, cc_ref, st_ref, w_ref, y_ref, ns_ref, *, ls, bw):
    u_ext = jnp.concatenate([st_ref[...], cc_ref[...] * ch_ref[...]], axis=1)
    cb = cb_ref[...]
    ys = []
    for t in range(ls):
        acc = None
        for j in range(CONV_W):
            term = w_ref[j:j + 1, :] * u_ext[:, (t + j) * bw:(t + j + 1) * bw]
            acc = term if acc is None else acc + term
        ys.append(cb[:, t * bw:(t + 1) * bw] * acc)
    y_ref[...] = jnp.concatenate(ys, axis=1).astype(y_ref.dtype)
    ns_ref[...] = u_ext[:, ls * bw:(ls + CONV_W - 1) * bw]


def _conv_sample(ch, cb, cc, state, conv_w, layer, bs, ls, bw):
    kern = functools.partial(_conv_sample_kernel, ls=ls, bw=bw)
    full = lambda shape: pl.BlockSpec(shape, lambda i: (0,) * len(shape))
    return pl.pallas_call(
        kern,
        grid=(1,),
        in_specs=[full((bs, ls * bw)), full((bs, ls * bw)), full((bs, ls * bw)),
                  pl.BlockSpec((None, bs, (CONV_W - 1) * bw), lambda i: (layer, 0, 0)),
                  pl.BlockSpec((None, CONV_W, bw), lambda i: (layer, 0, 0))],
        out_specs=[full((bs, ls * bw)), full((bs, (CONV_W - 1) * bw))],
        out_shape=[jax.ShapeDtypeStruct((bs, ls * bw), BF16),
                   jax.ShapeDtypeStruct((bs, (CONV_W - 1) * bw), F32)],
        compiler_params=_params(("arbitrary",)),
        name="conv_sample",
    )(ch, cb, cc, state, conv_w)


def _f32_key(x):
    bits = lax.bitcast_convert_type(x, I32)
    return jnp.where(bits < 0, bits ^ jnp.int32(0x7FFFFFFF), bits)


def _topk_select(key, col, tau, jmax):
    return (key > tau) | ((key == tau) & (col <= jmax))


def _topk_threshold(get_chunk, nch, k, n_valid, rows, width, jmax_ref):
    def count(pred):
        def body(c, part):
            m = jnp.where(pred(c, get_chunk(c)), 1.0, 0.0)
            blocks = [m[:, b * LANES:(b + 1) * LANES] for b in range(width // LANES)]
            while len(blocks) > 1:
                blocks = [a + b for a, b in zip(blocks[0::2], blocks[1::2])] + blocks[len(blocks) & ~1:]
            return part + blocks[0]

        part = lax.fori_loop(0, nch, body, jnp.zeros((rows, LANES), F32))
        return jnp.sum(part, axis=1, keepdims=True)

    kf = float(k)
    take_all = n_valid <= kf
    cnt0 = count(lambda c, kb: kb >= 0)
    base0 = jnp.where(cnt0 >= kf, jnp.int32(0), jnp.int32(INT_MIN))
    done0 = jnp.where(take_all | (cnt0 == kf), 1, 0).astype(I32)

    def left(done):
        return jnp.sum(jnp.where(done > 0, 0.0, 1.0))

    def cond(st):
        it, _, _, nleft = st
        return (it < 31) & (nleft > 0.0)

    def body(st):
        it, base, done, _ = st
        for u in range(BISECT_STEPS_PER_CHECK):
            bit = 30 - it - u
            cand = base | jnp.where(bit >= 0, jnp.left_shift(jnp.int32(1), jnp.maximum(bit, 0)), 0)
            cnt = count(lambda c, kb, cand=cand: kb >= cand)
            base = jnp.where(cnt >= kf, cand, base)
            done = jnp.where(cnt == kf, 1, done)
        return it + BISECT_STEPS_PER_CHECK, base, done, left(done)

    _, base, done, nleft = lax.while_loop(cond, body, (jnp.int32(0), base0, done0, left(done0)))
    tau = jnp.where(take_all, jnp.int32(INT_MIN), base)
    jmax_ref[...] = jnp.full(jmax_ref.shape, 2 ** 31 - 1, I32)

    @pl.when(nleft > 0.0)
    def _():
        lane = lax.broadcasted_iota(I32, (rows, width), 1)
        need = kf - count(lambda c, kb: kb > tau)

        def idx_body(it, lo):
            cand = lo | jnp.left_shift(jnp.int32(1), 30 - it)
            cnt = count(lambda c, kb: (kb == tau) & (c * width + lane < cand))
            return jnp.where(cnt < need, cand, lo)

        lo = lax.fori_loop(0, 31, idx_body, jnp.zeros((rows, 1), I32))
        jmax = jnp.where(done > 0, jnp.int32(2 ** 31 - 1), lo)
        jmax_ref[...] = jnp.broadcast_to(jmax, jmax_ref.shape)

    return tau, jmax_ref[:, 0:1]


def _dsa_prompt_kernel(iq_ref, mw_ref, aq_ref, ik_ref, ak_ref, av_ref, o_ref, sc_ref, jm_ref,
                       *, tq, tk, n_sel, nh, scale):
    i = pl.program_id(0)
    nch = ((i + 1) * tq + tk - 1) // tk
    row = i * tq + lax.broadcasted_iota(I32, (tq, tk), 0)
    col0 = lax.broadcasted_iota(I32, (tq, tk), 1)
    w = mw_ref[...]

    def score_body(c, carry):
        off = pl.multiple_of(c * tk, tk)
        ikc = ik_ref[pl.ds(off, tk), :]
        acc = jnp.zeros((tq, tk), F32)
        for h in range(IDX_HEADS):
            d = _dot_nt(iq_ref[:, h * LANES:(h + 1) * LANES], ikc)
            acc = acc + w[:, h:h + 1] * jnp.maximum(d, 0.0)
        s = jnp.where(off + col0 <= row, acc + 0.0, NEG_BIG)
        sc_ref[c] = _f32_key(s)
        return carry

    lax.fori_loop(0, nch, score_body, 0)
    n_valid = (row[:, 0:1] + 1).astype(F32)
    tau, jmax = _topk_threshold(lambda c: sc_ref[c], nch, n_sel, n_valid, tq, tk, jm_ref)

    def att_body(c, carry):
        off = pl.multiple_of(c * tk, tk)
        col = off + col0
        bias = jnp.where(_topk_select(sc_ref[c], col, tau, jmax) & (col <= row), 0.0, NEG_BIG)
        new = []
        for h in range(nh):
            m, l, acc = carry[h]
            hs = slice(h * HEAD_DIM, (h + 1) * HEAD_DIM)
            s = _dot_nt(aq_ref[:, hs], ak_ref[pl.ds(off, tk), hs]) * scale + bias
            m_new = jnp.maximum(m, jnp.max(s, axis=1, keepdims=True))
            p = jnp.exp(s - m_new)
            alpha = jnp.exp(m - m_new)
            l = l * alpha + jnp.sum(p, axis=1, keepdims=True)
            acc = acc * alpha + _dot(p.astype(BF16), av_ref[pl.ds(off, tk), hs])
            new.append((m_new, l, acc))
        return tuple(new)

    init = tuple((jnp.full((tq, 1), NEG_BIG, F32), jnp.zeros((tq, 1), F32), jnp.zeros((tq, HEAD_DIM), F32))
                 for _ in range(nh))
    fin = lax.fori_loop(0, nch, att_body, init)
    o_ref[...] = jnp.concatenate([acc / l for _, l, acc in fin], axis=1).astype(o_ref.dtype)


def _dsa_prompt(iq_r, misc2, aq_r, ik_b, ak_b, av_b, tp, lay):
    bw = lay.bw
    tq = 256
    tk = _pick(tp, (512, 256, 128))
    n_sel = min(TOPK_MAX, tp // 4)
    kern = functools.partial(_dsa_prompt_kernel, tq=tq, tk=tk, n_sel=n_sel, nh=lay.nh,
                             scale=float(HEAD_DIM ** -0.5))
    res = lambda w: pl.BlockSpec((tp, w), lambda i: (0, 0))
    return pl.pallas_call(
        kern,
        grid=(tp // tq,),
        in_specs=[pl.BlockSpec((tq, lay.iqw), lambda i: (i, 0)),
                  pl.BlockSpec((tq, LANES), lambda i: (i, 0)),
                  pl.BlockSpec((tq, bw), lambda i: (i, 0)),
                  res(LANES), res(bw), res(bw)],
        out_specs=pl.BlockSpec((tq, bw), lambda i: (i, 0)),
        out_shape=jax.ShapeDtypeStruct((tp, bw), BF16),
        scratch_shapes=[pltpu.VMEM((tp // tk, tq, tk), I32), pltpu.VMEM((tq, LANES), I32)],
        compiler_params=_params(("arbitrary",)),
        name="dsa_prompt",
    )(iq_r, misc2, aq_r, ik_b, ak_b, av_b)


def _new_key_valid(shape, ls):
    t = lax.broadcasted_iota(I32, shape, 0) & (SAMPLE_ROWS - 1)
    c = lax.broadcasted_iota(I32, shape, 1)
    return (c <= t) & (c < ls)


def _page_spec(shape, layer, g, gi):
    return pl.BlockSpec((None, None) + shape, lambda b, j, pt: (layer, pt[b, j * g + gi], 0, 0))


def _dsa_sample_score_kernel(pt_ref, iq_ref, iw_ref, *rest, npg, g, page, ls):
    pool_refs = rest[:g]
    new_ref, key_ref, keyn_ref = rest[g:]
    j = pl.program_id(1)

    def index_keys(keys, valid):
        d = _dot_nt(iq_ref[:, 0:IDX_DIM], keys.astype(BF16))
        r = jnp.maximum(d, 0.0) * iw_ref[:, 0:1]
        s = r[0:SAMPLE_ROWS, :]
        for h in range(1, IDX_HEADS):
            s = s + r[h * SAMPLE_ROWS:(h + 1) * SAMPLE_ROWS, :]
        s = s + 0.0
        return _f32_key(s if valid is None else jnp.where(valid, s, NEG_BIG))

    key_ref[...] = index_keys(jnp.concatenate([r[...] for r in pool_refs], axis=0), None)

    @pl.when(j == npg // g - 1)
    def _():
        keyn_ref[...] = index_keys(new_ref[...], _new_key_valid((SAMPLE_ROWS, page), ls))


def _dsa_sample_score(page_table, iq_s, iw_s, pool, ik_new, layer, ls):
    bs, npg = page_table.shape
    page = pool.shape[2]
    g = _pick(npg, (8, 4, 2, 1))
    kern = functools.partial(_dsa_sample_score_kernel, npg=npg, g=g, page=page, ls=ls)
    per_seq = lambda shape: pl.BlockSpec((None,) + shape, lambda b, j, pt: (b, 0, 0))
    grid_spec = pltpu.PrefetchScalarGridSpec(
        num_scalar_prefetch=1,
        grid=(bs, npg // g),
        in_specs=[per_seq((IDX_HEADS * SAMPLE_ROWS, LANES)), per_seq((IDX_HEADS * SAMPLE_ROWS, LANES))]
        + [_page_spec((page, IDX_DIM), layer, g, gi) for gi in range(g)]
        + [per_seq((page, IDX_DIM))],
        out_specs=[pl.BlockSpec((None, SAMPLE_ROWS, g * page), lambda b, j, pt: (b, 0, j)),
                   per_seq((SAMPLE_ROWS, page))],
    )
    return pl.pallas_call(
        kern,
        grid_spec=grid_spec,
        out_shape=[jax.ShapeDtypeStruct((bs, SAMPLE_ROWS, npg * page), I32),
                   jax.ShapeDtypeStruct((bs, SAMPLE_ROWS, page), I32)],
        compiler_params=_params(("parallel", "arbitrary")),
        name="dsa_sample_score",
    )(page_table, iq_s, iw_s, *([pool] * g), ik_new)


def _dsa_sample_thresh_kernel(key_ref, keyn_ref, tau_ref, sc_ref, jm_ref, *, nseq, past, ls, n_sel):
    rows = nseq * SAMPLE_ROWS
    width = sc_ref.shape[2]
    sc_ref[0] = jnp.concatenate([key_ref[...].reshape(rows, key_ref.shape[2]),
                                 keyn_ref[...].reshape(rows, keyn_ref.shape[2])], axis=1)
    t = lax.broadcasted_iota(I32, (rows, 1), 0) & (SAMPLE_ROWS - 1)
    n_valid = (past + jnp.minimum(t + 1, ls)).astype(F32)
    tau, jmax = _topk_threshold(lambda c: sc_ref[c], 1, n_sel, n_valid, rows, width, jm_ref)
    lane = lax.broadcasted_iota(I32, (rows, LANES), 1)
    tau_ref[...] = jnp.where(lane == 0, tau, jmax).reshape(tau_ref.shape)


def _dsa_sample_thresh(keys, keys_new, ls):
    bs, _, past = keys.shape
    page = keys_new.shape[2]
    nseq = _pick(bs, (4, 2, 1))
    n_sel = min(TOPK_MAX, (past + ls) // 4)
    kern = functools.partial(_dsa_sample_thresh_kernel, nseq=nseq, past=past, ls=ls, n_sel=n_sel)
    blk = lambda w: pl.BlockSpec((nseq, SAMPLE_ROWS, w), lambda b: (b, 0, 0))
    return pl.pallas_call(
        kern,
        grid=(bs // nseq,),
        in_specs=[blk(past), blk(page)],
        out_specs=blk(LANES),
        out_shape=jax.ShapeDtypeStruct((bs, SAMPLE_ROWS, LANES), I32),
        scratch_shapes=[pltpu.VMEM((1, nseq * SAMPLE_ROWS, past + page), I32),
                        pltpu.VMEM((nseq * SAMPLE_ROWS, LANES), I32)],
        compiler_params=_params(("parallel",)),
        name="dsa_sample_thresh",
    )(keys, keys_new)


def _stack_heads(q, nh):
    return jnp.concatenate([q[:, h * HEAD_DIM:(h + 1) * HEAD_DIM] for h in range(nh)], axis=0)


def _unstack_heads(o, nh):
    return jnp.concatenate([o[h * SAMPLE_ROWS:(h + 1) * SAMPLE_ROWS, :] for h in range(nh)], axis=1)


def _page_patterns(page, nh):
    shift = nh.bit_length() - 1
    assert 1 << shift == nh
    shape = (nh * SAMPLE_ROWS, page * nh)
    r = lax.broadcasted_iota(I32, shape, 0)
    c = lax.broadcasted_iota(I32, shape, 1)
    own_head = (c & (nh - 1)) == (r >> 3)
    new_valid = lambda ls: ((c >> shift) <= (r & (SAMPLE_ROWS - 1))) & ((c >> shift) < ls)
    er = lax.broadcasted_iota(I32, (page, page * nh), 0)
    ec = lax.broadcasted_iota(I32, (page, page * nh), 1)
    expand = jnp.where((ec >> shift) == er, 1.0, 0.0)
    return own_head, new_valid, expand


def _online_softmax_step(logits, masks, values, m_ref, l_ref, acc_ref):
    m = m_ref[...]
    m_new = m
    for s in logits:
        m_new = jnp.maximum(m_new, jnp.max(s, axis=1, keepdims=True))
    alpha = jnp.exp(m - m_new)
    l = l_ref[...] * alpha
    acc = acc_ref[...] * alpha
    for s, msk, v in zip(logits, masks, values):
        p = jnp.where(msk, jnp.exp(s - m_new), 0.0)
        l = l + jnp.sum(p, axis=1, keepdims=True)
        acc = acc + _dot(p.astype(BF16), v)
    l_ref[...] = l
    acc_ref[...] = acc
    m_ref[...] = m_new


def _sample_attn_init(q_ref, q2_ref, m_ref, l_ref, acc_ref, nh):
    q2_ref[...] = _stack_heads(q_ref[...], nh).astype(BF16)
    m_ref[...] = jnp.full_like(m_ref, NEG_BIG)
    l_ref[...] = jnp.zeros_like(l_ref)
    acc_ref[...] = jnp.zeros_like(acc_ref)


def _sample_attn_scratch(nh):
    rows = nh * SAMPLE_ROWS
    return [pltpu.VMEM((rows, HEAD_DIM), BF16), pltpu.VMEM((rows, 1), F32),
            pltpu.VMEM((rows, 1), F32), pltpu.VMEM((rows, HEAD_DIM), F32)]


def _dsa_sample_attn_kernel(pt_ref, q_ref, key_ref, keyn_ref, tau_ref, *rest, npg, g, page, ls, nh, scale):
    kp_refs, vp_refs = rest[:g], rest[g:2 * g]
    kn_ref, vn_ref, o_ref, q2_ref, m_ref, l_ref, acc_ref = rest[2 * g:]
    j = pl.program_id(1)

    @pl.when(j == 0)
    def _():
        _sample_attn_init(q_ref, q2_ref, m_ref, l_ref, acc_ref, nh)

    own_head, new_valid, expand = _page_patterns(page, nh)
    expand = expand.astype(BF16)
    tau, jmax = tau_ref[:, 0:1], tau_ref[:, 1:2]
    lane = lax.broadcasted_iota(I32, (SAMPLE_ROWS, page), 1)

    def attend(kps, vps, key8s, col0s, valid):
        sel8 = jnp.concatenate([jnp.where(_topk_select(k8, c0 + lane, tau, jmax), 1.0, 0.0)
                                for k8, c0 in zip(key8s, col0s)], axis=0).astype(BF16)
        sel_x = _dot(sel8, expand) > 0.5
        q2 = q2_ref[...]
        logits, masks = [], []
        for n, kp in enumerate(kps):
            sel = jnp.concatenate([sel_x[n * SAMPLE_ROWS:(n + 1) * SAMPLE_ROWS, :]] * nh, axis=0)
            msk = sel & own_head if valid is None else sel & own_head & valid
            logits.append(jnp.where(msk, _dot_nt(q2, kp.astype(BF16)) * scale, NEG_BIG))
            masks.append(msk)
        _online_softmax_step(logits, masks, [vp.astype(BF16) for vp in vps], m_ref, l_ref, acc_ref)

    attend([r[...] for r in kp_refs], [r[...] for r in vp_refs],
           [key_ref[:, gi * page:(gi + 1) * page] for gi in range(g)],
           [(j * g + gi) * page for gi in range(g)], None)

    @pl.when(j == npg // g - 1)
    def _():
        attend([kn_ref[...]], [vn_ref[...]], [keyn_ref[...]], [npg * page], new_valid(ls))
        o_ref[...] = _unstack_heads(acc_ref[...] / l_ref[...], nh).astype(o_ref.dtype)


def _dsa_sample_attn(page_table, q_s, keys, keys_new, tau, kpool, vpool, k_new, v_new, layer, ls, lay):
    bs, npg = page_table.shape
    rows = kpool.shape[2]
    bw, nh = lay.bw, lay.nh
    page = rows // nh
    g = _pick(npg, (8, 4, 2, 1))
    kern = functools.partial(_dsa_sample_attn_kernel, npg=npg, g=g, page=page, ls=ls, nh=nh,
                             scale=float(HEAD_DIM ** -0.5))
    per_seq = lambda shape: pl.BlockSpec((None,) + shape, lambda b, j, pt: (b, 0, 0))
    pages = [_page_spec((rows, HEAD_DIM), layer, g, gi) for gi in range(g)]
    grid_spec = pltpu.PrefetchScalarGridSpec(
        num_scalar_prefetch=1,
        grid=(bs, npg // g),
        in_specs=[per_seq((SAMPLE_ROWS, bw)),
                  pl.BlockSpec((None, SAMPLE_ROWS, g * page), lambda b, j, pt: (b, 0, j)),
                  per_seq((SAMPLE_ROWS, page)), per_seq((SAMPLE_ROWS, LANES))]
        + pages + pages + [per_seq((rows, HEAD_DIM)), per_seq((rows, HEAD_DIM))],
        out_specs=per_seq((SAMPLE_ROWS, bw)),
        scratch_shapes=_sample_attn_scratch(nh),
    )
    return pl.pallas_call(
        kern,
        grid_spec=grid_spec,
        out_shape=jax.ShapeDtypeStruct((bs, SAMPLE_ROWS, bw), BF16),
        compiler_params=_params(("parallel", "arbitrary")),
        name="dsa_sample_attn",
    )(page_table, q_s, keys, keys_new, tau, *([kpool] * g), *([vpool] * g), k_new, v_new)


def _upper_ones(n):
    r = lax.broadcasted_iota(I32, (n, n), 0)
    c = lax.broadcasted_iota(I32, (n, n), 1)
    return jnp.where(r <= c, 1.0, 0.0).astype(F32)


def _fox_cumsum_kernel(m2_ref, crow_ref, ccol_ref, carry_ref):
    @pl.when(pl.program_id(0) == 0)
    def _():
        carry_ref[...] = jnp.zeros_like(carry_ref)

    lf_t = m2_ref[...].T
    tc = lf_t.shape[1]
    cs = _dot(lf_t, _upper_ones(tc), HIGHEST) + carry_ref[:, 0:1]
    crow_ref[...] = cs
    ccol_ref[...] = cs.T
    carry_ref[...] = jnp.broadcast_to(cs[:, tc - 1:tc], carry_ref.shape)


def _fox_cumsum(misc2, tp):
    tc = _pick(tp, (512, 256, 128))
    return pl.pallas_call(
        _fox_cumsum_kernel,
        grid=(tp // tc,),
        in_specs=[pl.BlockSpec((tc, LANES), lambda i: (i, 0))],
        out_specs=[pl.BlockSpec((LANES, tc), lambda i: (0, i)), pl.BlockSpec((tc, LANES), lambda i: (i, 0))],
        out_shape=[jax.ShapeDtypeStruct((LANES, tp), F32), jax.ShapeDtypeStruct((tp, LANES), F32)],
        scratch_shapes=[pltpu.VMEM((LANES, LANES), F32)],
        compiler_params=_params(("arbitrary",)),
        name="fox_cumsum",
    )(misc2)


def _fox_prompt_kernel(q_ref, k_ref, v_ref, ccol_ref, crow_ref, o_ref, m_ref, l_ref, acc_ref,
                       *, tq, tk, nh, ff_lane, scale):
    i, j = pl.program_id(0), pl.program_id(1)

    @pl.when(j == 0)
    def _():
        m_ref[...] = jnp.full_like(m_ref, NEG_BIG)
        l_ref[...] = jnp.zeros_like(l_ref)
        acc_ref[...] = jnp.zeros_like(acc_ref)

    def block(diagonal):
        def sub(r, carry):
            rows = pl.ds(pl.multiple_of(r * sub_q, sub_q), sub_q)
            if diagonal:
                below = lax.broadcasted_iota(I32, (sub_q, tk), 1) <= r * sub_q + lax.broadcasted_iota(I32, (sub_q, tk), 0)
            for h in range(nh):
                hs = slice(h * HEAD_DIM, (h + 1) * HEAD_DIM)
                s = _dot_nt(q_ref[rows, hs], k_ref[:, hs]) * scale
                s = s + ccol_ref[rows, ff_lane + h:ff_lane + h + 1] - crow_ref[ff_lane + h:ff_lane + h + 1, :]
                if diagonal:
                    s = jnp.where(below, s, NEG_BIG)
                m = m_ref[h, rows, :]
                m_new = jnp.maximum(m, jnp.max(s, axis=1, keepdims=True))
                p = jnp.exp(s - m_new[:, 0:1])
                alpha = jnp.exp(m - m_new)
                l_ref[h, rows, :] = l_ref[h, rows, :] * alpha + jnp.sum(p, axis=1, keepdims=True)
                acc_ref[h, rows, :] = acc_ref[h, rows, :] * alpha + _dot(p.astype(BF16), v_ref[:, hs])
                m_ref[h, rows, :] = m_new
            return carry

        lax.fori_loop(0, tq // sub_q, sub, 0)

    sub_q = min(tq, 256)

    @pl.when(j < i)
    def _():
        block(False)

    @pl.when(j == i)
    def _():
        block(True)
        o_ref[...] = jnp.concatenate([acc_ref[h] / l_ref[h] for h in range(nh)], axis=1).astype(o_ref.dtype)


def _fox_prompt(fq_b, fk_b, fv_b, ccol, crow, tp, lay):
    bw = lay.bw
    tq = tk = _pick(tp, (512, 256, 128))
    kern = functools.partial(_fox_prompt_kernel, tq=tq, tk=tk, nh=lay.nh, ff_lane=lay.ff_lane,
                             scale=float(HEAD_DIM ** -0.5))
    return pl.pallas_call(
        kern,
        grid=(tp // tq, tp // tk),
        in_specs=[pl.BlockSpec((tq, bw), lambda i, j: (i, 0)),
                  pl.BlockSpec((tk, bw), lambda i, j: (jnp.minimum(j, i), 0)),
                  pl.BlockSpec((tk, bw), lambda i, j: (jnp.minimum(j, i), 0)),
                  pl.BlockSpec((tq, LANES), lambda i, j: (i, 0)),
                  pl.BlockSpec((LANES, tk), lambda i, j: (0, jnp.minimum(j, i)))],
        out_specs=pl.BlockSpec((tq, bw), lambda i, j: (i, 0)),
        out_shape=jax.ShapeDtypeStruct((tp, bw), BF16),
        scratch_shapes=[pltpu.VMEM((lay.nh, tq, LANES), F32), pltpu.VMEM((lay.nh, tq, LANES), F32),
                        pltpu.VMEM((lay.nh, tq, HEAD_DIM), F32)],
        compiler_params=_params(("parallel", "arbitrary")),
        name="fox_prompt",
    )(fq_b, fk_b, fv_b, ccol, crow)


def _fox_sample_kernel(pt_ref, q_ref, *rest, npg, g, page, ls, nh, scale):
    kp_refs, vp_refs, lf_refs = rest[:g], rest[g:2 * g], rest[2 * g:3 * g]
    kn_ref, vn_ref, lfn_ref, o_ref, carry_ref, q2_ref, m_ref, l_ref, acc_ref = rest[3 * g:]
    j = pl.program_id(1)

    @pl.when(j == 0)
    def _():
        _sample_attn_init(q_ref, q2_ref, m_ref, l_ref, acc_ref, nh)
        carry_ref[...] = jnp.zeros_like(carry_ref)

    own_head, new_valid, expand = _page_patterns(page, nh)
    upper = _upper_ones(page)

    def attend(kps, vps, lfs, valid):
        within = _dot(jnp.concatenate(lfs, axis=0), upper, HIGHEST)
        carry = carry_ref[:, 0:1]
        cs = []
        for n in range(len(lfs)):
            w = within[n * SAMPLE_ROWS:(n + 1) * SAMPLE_ROWS, :]
            cs.append(w + carry)
            carry = carry + w[:, page - 1:page]
        carry_ref[...] = jnp.broadcast_to(carry, carry_ref.shape)
        cs_x = _dot(jnp.concatenate(cs, axis=0), expand, HIGHEST)
        q2 = q2_ref[...]
        msk = own_head if valid is None else own_head & valid
        logits = []
        for n, kp in enumerate(kps):
            bias = jnp.concatenate([jnp.broadcast_to(cs_x[n * SAMPLE_ROWS + h:n * SAMPLE_ROWS + h + 1, :],
                                                     (SAMPLE_ROWS, page * nh)) for h in range(nh)], axis=0)
            logits.append(jnp.where(msk, _dot_nt(q2, kp.astype(BF16)) * scale - bias, NEG_BIG))
        _online_softmax_step(logits, [msk] * len(kps), [vp.astype(BF16) for vp in vps], m_ref, l_ref, acc_ref)

    attend([r[...] for r in kp_refs], [r[...] for r in vp_refs], [r[...] for r in lf_refs], None)

    @pl.when(j == npg // g - 1)
    def _():
        attend([kn_ref[...]], [vn_ref[...]], [lfn_ref[...]], new_valid(ls))
        o_ref[...] = _unstack_heads(acc_ref[...] / l_ref[...], nh).astype(o_ref.dtype)


def _fox_sample(page_table, q_s, kpool, vpool, lfpool_t, k_new, v_new, lf_new, layer, ls, lay):
    bs, npg = page_table.shape
    rows = kpool.shape[2]
    bw, nh = lay.bw, lay.nh
    page = rows // nh
    g = _pick(npg, (8, 4, 2, 1))
    kern = functools.partial(_fox_sample_kernel, npg=npg, g=g, page=page, ls=ls, nh=nh,
                             scale=float(HEAD_DIM ** -0.5))
    per_seq = lambda shape: pl.BlockSpec((None,) + shape, lambda b, j, pt: (b, 0, 0))
    pages = [_page_spec((rows, HEAD_DIM), layer, g, gi) for gi in range(g)]
    grid_spec = pltpu.PrefetchScalarGridSpec(
        num_scalar_prefetch=1,
        grid=(bs, npg // g),
        in_specs=[per_seq((SAMPLE_ROWS, bw))] + pages + pages
        + [_page_spec((SAMPLE_ROWS, page), layer, g, gi) for gi in range(g)]
        + [per_seq((rows, HEAD_DIM)), per_seq((rows, HEAD_DIM)), per_seq((SAMPLE_ROWS, page))],
        out_specs=per_seq((SAMPLE_ROWS, bw)),
        scratch_shapes=[pltpu.VMEM((SAMPLE_ROWS, LANES), F32)] + _sample_attn_scratch(nh),
    )
    return pl.pallas_call(
        kern,
        grid_spec=grid_spec,
        out_shape=jax.ShapeDtypeStruct((bs, SAMPLE_ROWS, bw), BF16),
        compiler_params=_params(("parallel", "arbitrary")),
        name="fox_sample",
    )(page_table, q_s, *([kpool] * g), *([vpool] * g), *([lfpool_t] * g), k_new, v_new, lf_new)


def _hgrn_lower_bound(lg_ref, layer):
    lg = lg_ref[...]
    e = jnp.exp(lg - jnp.max(lg, axis=0, keepdims=True))
    p = e / jnp.sum(e, axis=0, keepdims=True)
    lb = jnp.zeros((1, lg.shape[1]), F32)
    for i in range(1, layer + 1):
        lb = lb + p[i:i + 1, :]
    return lb


def _hgrn_rows(hq, hf, hi, lb, chunk, row_valid, nh):
    r = hq.shape[0]
    pos = lax.broadcasted_iota(I32, hq.shape, 0) & (chunk - 1)
    lf = jnp.log(lb + (1.0 - lb) * jax.nn.sigmoid(hf))
    kk = (1.0 - lb) * jax.nn.sigmoid(-hf)
    if row_valid is not None:
        lf = jnp.where(row_valid, lf, 0.0)
        kk = jnp.where(row_valid, kk, 0.0)
    b = lf
    step = 1
    while step < chunk:
        b = b + jnp.where(pos >= step, pltpu.roll(b, step, 0), 0.0)
        step *= 2
    suf = lf
    step = 1
    while step < chunk:
        suf = suf + jnp.where(pos + step < chunk, pltpu.roll(suf, r - step, 0), 0.0)
        step *= 2
    tail = suf - lf
    qe = hq * jnp.exp(b)
    kd = kk * jnp.exp(tail)
    etot = jnp.exp(b + tail)
    o_intra = jnp.zeros_like(hq)
    for d in range(chunk):
        if d == 0:
            kk_d, b_d, i_d = kk, b, hi
        else:
            kk_d, b_d, i_d = pltpu.roll(kk, d, 0), pltpu.roll(b, d, 0), pltpu.roll(hi, d, 0)
        dec = jnp.exp(jnp.where(pos >= d, b - b_d, NEG_BIG))
        prod = hq * dec * kk_d
        a = jnp.concatenate(
            [jnp.broadcast_to(jnp.sum(prod[:, h * HEAD_DIM:(h + 1) * HEAD_DIM], axis=1, keepdims=True), (r, HEAD_DIM))
             for h in range(nh)], axis=1)
        o_intra = o_intra + a * i_d
    return qe, kd, etot, o_intra


def _hgrn_finish(o, hg, ng, nh):
    outs = []
    for h in range(nh):
        oh = o[:, h * HEAD_DIM:(h + 1) * HEAD_DIM]
        outs.append(oh * lax.rsqrt(jnp.mean(oh * oh, axis=1, keepdims=True) + EPS))
    return jnp.concatenate(outs, axis=1) * ng * (hg * jax.nn.sigmoid(hg))


def _hgrn_prompt_kernel(hq_ref, hf_ref, hi_ref, hg_ref, lg_ref, ng_ref, y_ref, s_ref,
                        st_ref, qe_ref, kd_ref, et_ref, oi_ref, *, layer, chunk, nh):
    @pl.when(pl.program_id(0) == 0)
    def _():
        st_ref[...] = jnp.zeros_like(st_ref)

    lb = _hgrn_lower_bound(lg_ref, layer)
    hi = hi_ref[...]
    qe, kd, etot, o_intra = _hgrn_rows(hq_ref[...], hf_ref[...], hi, lb, chunk, None, nh)
    qe_ref[...] = qe.astype(BF16)
    kd_ref[...] = kd.astype(BF16)
    et_ref[...] = etot
    tm = hi.shape[0]

    def chunk_body(c, carry):
        rows = pl.ds(pl.multiple_of(c * chunk, chunk), chunk)
        for h in range(nh):
            hs = slice(h * HEAD_DIM, (h + 1) * HEAD_DIM)
            st = st_ref[h]
            oi_ref[rows, hs] = _dot_nt(qe_ref[rows, hs], st.astype(BF16))
            e = et_ref[rows, hs]
            st_ref[h] = st * e[0:1, :] + _dot_tn(hi_ref[rows, hs].astype(BF16), kd_ref[rows, hs])
        return carry

    lax.fori_loop(0, tm // chunk, chunk_body, 0)
    y_ref[...] = _hgrn_finish(oi_ref[...] + o_intra, hg_ref[...], ng_ref[...], nh).astype(y_ref.dtype)

    @pl.when(pl.program_id(0) == pl.num_programs(0) - 1)
    def _():
        for h in range(nh):
            s_ref[h] = st_ref[h].T


def _hgrn_prompt(z, lb_logits, norm_g, tp, lay, layer):
    bw, nh = lay.bw, lay.nh
    tm = _pick(tp, (512, 256, 128))
    dep = lb_logits.shape[0]
    col = lambda off: pl.BlockSpec((tm, bw), lambda i: (i, off // bw))
    kern = functools.partial(_hgrn_prompt_kernel, layer=layer, chunk=HG_CHUNK_PROMPT, nh=nh)
    return pl.pallas_call(
        kern,
        grid=(tp // tm,),
        in_specs=[col(lay.hq), col(lay.hf), col(lay.hi), col(lay.hg),
                  pl.BlockSpec((dep, bw), lambda i: (0, 0)),
                  pl.BlockSpec((None, 1, bw), lambda i: (layer, 0, 0))],
        out_specs=[pl.BlockSpec((tm, bw), lambda i: (i, 0)),
                   pl.BlockSpec((nh, HEAD_DIM, HEAD_DIM), lambda i: (0, 0, 0))],
        out_shape=[jax.ShapeDtypeStruct((tp, bw), BF16), jax.ShapeDtypeStruct((nh, HEAD_DIM, HEAD_DIM), F32)],
        scratch_shapes=[pltpu.VMEM((nh, HEAD_DIM, HEAD_DIM), F32), pltpu.VMEM((tm, bw), BF16),
                        pltpu.VMEM((tm, bw), BF16), pltpu.VMEM((tm, bw), F32), pltpu.VMEM((tm, bw), F32)],
        compiler_params=_params(("arbitrary",)),
        name="hgrn_prompt",
    )(z, z, z, z, lb_logits, norm_g)


def _hgrn_sample_kernel(hq_ref, hf_ref, hi_ref, hg_ref, lg_ref, ng_ref, s0_ref, y_ref, s_ref,
                        *, layer, ls, nh):
    lb = _hgrn_lower_bound(lg_ref, layer)
    hi = hi_ref[...]
    row_valid = lax.broadcasted_iota(I32, hi.shape, 0) < ls
    hi = jnp.where(row_valid, hi, 0.0)
    qe, kd, etot, o_intra = _hgrn_rows(hq_ref[...], hf_ref[...], hi, lb, SAMPLE_ROWS, row_valid, nh)
    outs = []
    for h in range(nh):
        hs = slice(h * HEAD_DIM, (h + 1) * HEAD_DIM)
        st = s0_ref[h].T
        outs.append(_dot_nt(qe[:, hs].astype(BF16), st.astype(BF16)))
        st = st * etot[0:1, hs] + _dot_tn(hi[:, hs].astype(BF16), kd[:, hs].astype(BF16))
        s_ref[h] = st.T
    o = jnp.concatenate(outs, axis=1) + o_intra
    y_ref[...] = _hgrn_finish(o, hg_ref[...], ng_ref[...], nh).astype(y_ref.dtype)


def _hgrn_sample(hq, hf, hi, hg, lb_logits, norm_g, state, layer, ls, lay):
    bs = hq.shape[0]
    bw, nh = lay.bw, lay.nh
    dep = lb_logits.shape[0]
    rows = pl.BlockSpec((None, SAMPLE_ROWS, bw), lambda b: (b, 0, 0))
    kern = functools.partial(_hgrn_sample_kernel, layer=layer, ls=ls, nh=nh)
    return pl.pallas_call(
        kern,
        grid=(bs,),
        in_specs=[rows, rows, rows, rows,
                  pl.BlockSpec((dep, bw), lambda b: (0, 0)),
                  pl.BlockSpec((None, 1, bw), lambda b: (layer, 0, 0)),
                  pl.BlockSpec((None, None, nh, HEAD_DIM, HEAD_DIM), lambda b: (layer, b, 0, 0, 0))],
        out_specs=[rows, pl.BlockSpec((None, nh, HEAD_DIM, HEAD_DIM), lambda b: (b, 0, 0, 0))],
        out_shape=[jax.ShapeDtypeStruct((bs, SAMPLE_ROWS, bw), BF16),
                   jax.ShapeDtypeStruct((bs, nh, HEAD_DIM, HEAD_DIM), F32)],
        compiler_params=_params(("parallel",)),
        name="hgrn_sample",
    )(hq, hf, hi, hg, lb_logits, norm_g, state)


def _pad_rows(a, bs, ls, rows):
    return jnp.pad(a.reshape(bs, ls, a.shape[-1]), ((0, 0), (0, rows - ls), (0, 0)))


def kernel(x_prompt, x_sample, cache_dsa_k, cache_dsa_v, cache_dsa_kidx, cache_fox_k, cache_fox_v,
           cache_fox_logf, state_conv, state_hgrn, page_table, norm_mix_g, w_in, conv_w, fox_fb,
           hgrn_lb_logits, hgrn_norm_g, w_branch, w_out, norm_ffn_g, w_up, w_down, final_norm_g):
    bp, tp, d = x_prompt.shape
    bs, ls, _ = x_sample.shape
    assert bp == 1 and ls <= SAMPLE_ROWS
    depth = w_in.shape[0]
    lay = _Layout(d)
    bw, nh = lay.bw, lay.nh
    ts = bs * ls
    npg = page_table.shape[1]
    n_pool, page = cache_dsa_k.shape[1], cache_dsa_k.shape[2]
    assert page == LANES and tp % LANES == 0
    past = npg * page

    pos = jnp.concatenate([jnp.arange(tp), past + jnp.tile(jnp.arange(ls), bs)])
    tab_a, half_a = _rope_tables(pos, HEAD_DIM)
    tab_i, half_i = _rope_tables(pos, IDX_DIM)

    w_in_r = _relayout_w_in(w_in, lay)
    w_br_b, w_out_b = w_branch.astype(BF16), w_out.astype(BF16)
    w_up_b, w_down_b = w_up.astype(BF16), w_down.astype(BF16)
    fb_rows = jnp.pad(fox_fb, ((0, 0), (lay.ff_lane, LANES - lay.ff_lane - nh)))[:, None, :]
    g_mix, g_ffn, g_hg = norm_mix_g[:, None, :], norm_ffn_g[:, None, :], hgrn_norm_g[:, None, :]
    as_pages = lambda c: c.reshape(depth, n_pool, page * nh, HEAD_DIM)
    kpool_a, vpool_a, kpool_f, vpool_f = map(as_pages, (cache_dsa_k, cache_dsa_v, cache_fox_k, cache_fox_v))
    new_page = lambda a: jnp.pad(a.reshape(bs, ls * nh, HEAD_DIM), ((0, 0), (0, (page - ls) * nh), (0, 0)))
    lfpool_t = jnp.pad(jnp.swapaxes(cache_fox_logf, 2, 3), ((0, 0), (0, 0), (0, SAMPLE_ROWS - nh), (0, 0)))
    conv_state = state_conv.reshape(depth, bs, (CONV_W - 1) * bw)

    x = jnp.concatenate([x_prompt[0], x_sample.reshape(ts, d)], axis=0)
    new_p, new_s = [], []
    for l in range(depth):
        z = _in_proj(x, g_mix, w_in_r, l)
        (aq_r, ak_r, ak_b, av_b, iq_r, ik_r, ik_b, misc2,
         fq_b, fk_b, fv_b) = _post(z, tab_a, tab_i, fb_rows, lay, l, half_a, half_i)
        zs = z[tp:]
        sec = lambda off: zs[:, off:off + bw]

        y_conv_p, u_last = _conv_prompt(z, conv_w, tp, lay, l)
        flat = lambda a: a.reshape(bs, ls * bw)
        y_conv_s, conv_new_s = _conv_sample(flat(sec(lay.ch)), flat(sec(lay.cb)), flat(sec(lay.cc)),
                                            conv_state, conv_w, l, bs, ls, bw)

        y_dsa_p = _dsa_prompt(iq_r, misc2, aq_r, ik_b, ak_b, av_b, tp, lay)
        iq_s = _pad_rows(iq_r[tp:], bs, ls, SAMPLE_ROWS).reshape(bs, SAMPLE_ROWS, IDX_HEADS, LANES)
        iq_s = jnp.swapaxes(iq_s, 1, 2).reshape(bs, IDX_HEADS * SAMPLE_ROWS, LANES)
        iw_s = jnp.swapaxes(_pad_rows(misc2[tp:, :IDX_HEADS], bs, ls, SAMPLE_ROWS), 1, 2)
        iw_s = jnp.broadcast_to(iw_s.reshape(bs, IDX_HEADS * SAMPLE_ROWS, 1), (bs, IDX_HEADS * SAMPLE_ROWS, LANES))
        ik_new = _pad_rows(ik_r[tp:, :IDX_DIM], bs, ls, page)
        keys, keys_new = _dsa_sample_score(page_table, iq_s, iw_s, cache_dsa_kidx, ik_new, l, ls)
        tau = _dsa_sample_thresh(keys, keys_new, ls)
        y_dsa_s = _dsa_sample_attn(page_table, _pad_rows(aq_r[tp:], bs, ls, SAMPLE_ROWS), keys, keys_new, tau,
                                   kpool_a, vpool_a, new_page(ak_b[tp:]), new_page(av_b[tp:]), l, ls, lay)

        crow, ccol = _fox_cumsum(misc2, tp)
        y_fox_p = _fox_prompt(fq_b, fk_b, fv_b, ccol, crow, tp, lay)
        logf_s = misc2[tp:, lay.ff_lane:lay.ff_lane + nh]
        lf_new = jnp.pad(jnp.swapaxes(logf_s.reshape(bs, ls, nh), 1, 2),
                         ((0, 0), (0, SAMPLE_ROWS - nh), (0, page - ls)))
        y_fox_s = _fox_sample(page_table, _pad_rows(fq_b[tp:], bs, ls, SAMPLE_ROWS), kpool_f, vpool_f, lfpool_t,
                              new_page(fk_b[tp:]), new_page(fv_b[tp:]), lf_new, l, ls, lay)

        y_hg_p, s_hg_p = _hgrn_prompt(z, hgrn_lb_logits, g_hg, tp, lay, l)
        pad8 = lambda off: _pad_rows(sec(off), bs, ls, SAMPLE_ROWS)
        y_hg_s, s_hg_s = _hgrn_sample(pad8(lay.hq), pad8(lay.hf), pad8(lay.hi), pad8(lay.hg),
                                      hgrn_lb_logits, g_hg, state_hgrn, l, ls, lay)

        unpad = lambda a: a[:, :ls].reshape(ts, bw)
        branches = [jnp.concatenate([y_conv_p, y_conv_s.reshape(ts, bw)], axis=0),
                    jnp.concatenate([y_dsa_p, unpad(y_dsa_s)], axis=0),
                    jnp.concatenate([y_fox_p, unpad(y_fox_s)], axis=0),
                    jnp.concatenate([y_hg_p, unpad(y_hg_s)], axis=0)]
        x = _mix(branches, z, w_br_b, w_out_b, x, lay, l)
        x = _ffn(x, g_ffn, w_up_b, w_down_b, l)

        heads = lambda a, b_, l_: a.reshape(b_, l_, nh, HEAD_DIM)
        zp = lambda off: z[:tp, off:off + bw]
        new_p.append((heads(ak_r[:tp], 1, tp), heads(zp(lay.av), 1, tp), ik_r[:tp, :IDX_DIM].reshape(1, tp, IDX_DIM),
                      heads(zp(lay.fk), 1, tp), heads(zp(lay.fv), 1, tp),
                      misc2[:tp, lay.ff_lane:lay.ff_lane + nh].reshape(1, tp, nh),
                      u_last[8 - (CONV_W - 1):].reshape(1, CONV_W - 1, bw), s_hg_p[None]))
        new_s.append((heads(ak_r[tp:], bs, ls), heads(sec(lay.av), bs, ls), ik_r[tp:, :IDX_DIM].reshape(bs, ls, IDX_DIM),
                      heads(sec(lay.fk), bs, ls), heads(sec(lay.fv), bs, ls), logf_s.reshape(bs, ls, nh),
                      conv_new_s.reshape(bs, CONV_W - 1, bw), s_hg_s))

    y = _final_norm(x, final_norm_g[None, :])
    y_prompt = y[:tp][None]
    y_sample = y[tp:].reshape(bs, ls, d)
    outs_p = [jnp.stack(t) for t in zip(*new_p)]
    outs_s = [jnp.stack(t) for t in zip(*new_s)]
    return (y_prompt, y_sample, *outs_p, *outs_s)
```

```python
import functools

import jax
import jax.numpy as jnp
from jax import lax
from jax.experimental import pallas as pl
from jax.experimental.pallas import tpu as pltpu

F32 = jnp.float32
BF16 = jnp.bfloat16
I32 = jnp.int32

LANES = 128
HEAD_DIM = 128
IDX_DIM = 64
IDX_HEADS = 16
CONV_W = 3
TOPK_MAX = 256
ROPE_THETA = 500000.0
ROT_FRACTION = 4
EPS = 1e-6
NEG_BIG = -1e30
INT_MIN = -2 ** 31
SAMPLE_ROWS = 8
HG_CHUNK_PROMPT = 16
BISECT_STEPS_PER_CHECK = 4
ATTN_PAGES_PER_STEP = (16, 8, 4, 2, 1)
VMEM_LIMIT = 56 * 1024 * 1024
HIGHEST = lax.Precision.HIGHEST


def _dot(a, b, precision=None):
    return jnp.dot(a, b, preferred_element_type=F32, precision=precision)


def _dot_nt(a, b):
    return lax.dot_general(a, b, (((1,), (1,)), ((), ())), preferred_element_type=F32)


def _dot_tn(a, b):
    return lax.dot_general(a, b, (((0,), (0,)), ((), ())), preferred_element_type=F32)


def _pick(n, cands):
    for c in cands:
        if n % c == 0:
            return c
    raise ValueError(f"no tile for {n} in {cands}")


def _params(sem, vmem=VMEM_LIMIT):
    return pltpu.CompilerParams(dimension_semantics=sem, vmem_limit_bytes=vmem)


def _log_sigmoid(x):
    return jnp.minimum(x, 0.0) - jnp.log1p(jnp.exp(-jnp.abs(x)))


def _rms(x, g):
    return x * lax.rsqrt(jnp.mean(x * x, axis=-1, keepdims=True) + EPS) * g


class _Layout:
    def __init__(self, d_model):
        bw = d_model // 4
        assert bw % HEAD_DIM == 0
        self.d = d_model
        self.bw = bw
        self.nh = bw // HEAD_DIM
        self.iqw = IDX_HEADS * LANES
        assert self.iqw % bw == 0 and IDX_HEADS + self.nh <= LANES
        o = 0
        for name in ("ch", "cb", "cc", "aq", "ak", "av", "fq", "fk"):
            setattr(self, name, o)
            o += bw
        self.iq = o
        o += self.iqw
        for name in ("fv", "hq", "hf", "hi", "hg"):
            setattr(self, name, o)
            o += bw
        self.ikp = o
        o += LANES
        self.misc = o
        o += LANES
        o = -(-o // bw) * bw
        self.gate = o
        o += 4 * d_model
        self.n = o
        self.ff_lane = IDX_HEADS


def _relayout_w_in(w_in, lay):
    bw, nh = lay.bw, lay.nh
    dep, d, _ = w_in.shape
    wt = jnp.transpose(w_in, (2, 0, 1))
    o = 0

    def take(width):
        nonlocal o
        s = wt[o:o + width]
        o += width
        return s

    first6 = take(6 * bw)
    iq = take(IDX_HEADS * IDX_DIM)
    ik = take(IDX_DIM)
    iw = take(IDX_HEADS)
    fq, fk, fv = take(bw), take(bw), take(bw)
    ff = take(nh)
    h4 = take(4 * bw)
    gate = take(4 * lay.d)
    assert o == wt.shape[0]
    iq_pad = jnp.pad(iq.reshape(IDX_HEADS, IDX_DIM, dep, d),
                     ((0, 0), (0, LANES - IDX_DIM), (0, 0), (0, 0))).reshape(lay.iqw, dep, d)
    ik_pad = jnp.pad(ik, ((0, LANES - IDX_DIM), (0, 0), (0, 0)))
    misc = jnp.pad(jnp.concatenate([iw, ff], axis=0), ((0, LANES - IDX_HEADS - nh), (0, 0), (0, 0)))
    pad = jnp.zeros((lay.gate - lay.misc - LANES, dep, d), w_in.dtype)
    out = jnp.concatenate([first6, fq, fk, iq_pad, fv, h4, ik_pad, misc, pad, gate], axis=0)
    assert out.shape[0] == lay.n
    return [out[:, l, :].astype(BF16) for l in range(dep)]


def _rope_tables(pos, head_dim):
    d_rot = head_dim // ROT_FRACTION
    half = d_rot // 2
    inv = ROPE_THETA ** (-jnp.arange(half, dtype=F32) / half)
    ang = pos.astype(F32)[:, None] * inv[None, :]
    cos, sin = jnp.cos(ang), jnp.sin(ang)
    t = pos.shape[0]
    rest = LANES - d_rot
    c = jnp.concatenate([cos, cos, jnp.ones((t, rest), F32)], axis=1)
    s1 = jnp.concatenate([-sin, jnp.zeros((t, half + rest), F32)], axis=1)
    s2 = jnp.concatenate([jnp.zeros((t, half), F32), sin, jnp.zeros((t, rest), F32)], axis=1)
    return jnp.concatenate([c, s1, s2], axis=1), half


def _in_proj_kernel(x_ref, g_ref, w_ref, z_ref, xn_ref):
    @pl.when(pl.program_id(1) == 0)
    def _():
        xn_ref[...] = _rms(x_ref[...], g_ref[...]).astype(BF16)

    z_ref[...] = _dot_nt(xn_ref[...], w_ref[...])


def _in_proj(x, g, w_r, layer):
    t, d = x.shape
    n = w_r.shape[0]
    tm = _pick(t, (640, 512, 256, 128))
    tn = _pick(n, (1024, 512))
    return pl.pallas_call(
        _in_proj_kernel,
        grid=(t // tm, n // tn),
        in_specs=[pl.BlockSpec((tm, d), lambda i, j: (i, 0)),
                  pl.BlockSpec((None, 1, d), lambda i, j: (layer, 0, 0)),
                  pl.BlockSpec((tn, d), lambda i, j: (j, 0))],
        out_specs=pl.BlockSpec((tm, tn), lambda i, j: (i, j)),
        out_shape=jax.ShapeDtypeStruct((t, n), F32),
        scratch_shapes=[pltpu.VMEM((tm, d), BF16)],
        compiler_params=_params(("parallel", "arbitrary")),
        name="in_proj",
    )(x, g, w_r)


def _rope(x, tab, half, reps):
    w = x.shape[1]
    c = jnp.concatenate([tab[:, 0:LANES]] * reps, axis=1)
    s1 = jnp.concatenate([tab[:, LANES:2 * LANES]] * reps, axis=1)
    s2 = jnp.concatenate([tab[:, 2 * LANES:3 * LANES]] * reps, axis=1)
    return x * c + pltpu.roll(x, w - half, 1) * s1 + pltpu.roll(x, half, 1) * s2


def _post_kernel(aq_ref, ak_ref, av_ref, iq_ref, ikp_ref, misc_ref, fq_ref, fk_ref, fv_ref,
                 ta_ref, ti_ref, fb_ref,
                 aqr_ref, akr_ref, akb_ref, avb_ref, iqr_ref, ikr_ref, ikb_ref, m2_ref,
                 fqb_ref, fkb_ref, fvb_ref, *, nh, half_a, half_i, idx_scale):
    ta, ti = ta_ref[...], ti_ref[...]
    fqb_ref[...] = fq_ref[...].astype(BF16)
    fkb_ref[...] = fk_ref[...].astype(BF16)
    fvb_ref[...] = fv_ref[...].astype(BF16)
    aqr_ref[...] = _rope(aq_ref[...], ta, half_a, nh).astype(BF16)
    akr = _rope(ak_ref[...], ta, half_a, nh)
    akr_ref[...] = akr
    akb_ref[...] = akr.astype(BF16)
    avb_ref[...] = av_ref[...].astype(BF16)
    iqr_ref[...] = _rope(iq_ref[...], ti, half_i, IDX_HEADS).astype(BF16)
    ikr = _rope(ikp_ref[...], ti, half_i, 1)
    ikr_ref[...] = ikr
    ikb_ref[...] = ikr.astype(BF16)
    m = misc_ref[...]
    lane = lax.broadcasted_iota(I32, m.shape, 1)
    logf = _log_sigmoid(m + fb_ref[...])
    m2_ref[...] = jnp.where(lane < IDX_HEADS, m * idx_scale,
                            jnp.where(lane < IDX_HEADS + nh, logf, 0.0))


def _post(z, tab_a, tab_i, fb_row, lay, layer, half_a, half_i):
    t = z.shape[0]
    bw = lay.bw
    tm = _pick(t, (320, 256, 128, 64, 32, 16))
    col = lambda off, w: pl.BlockSpec((tm, w), lambda i: (i, off // w))
    row = lambda w: pl.BlockSpec((tm, w), lambda i: (i, 0))
    kern = functools.partial(_post_kernel, nh=lay.nh, half_a=half_a, half_i=half_i,
                             idx_scale=float((IDX_HEADS * IDX_DIM) ** -0.5))
    return pl.pallas_call(
        kern,
        grid=(t // tm,),
        in_specs=[col(lay.aq, bw), col(lay.ak, bw), col(lay.av, bw), col(lay.iq, lay.iqw),
                  col(lay.ikp, LANES), col(lay.misc, LANES), col(lay.fq, bw), col(lay.fk, bw), col(lay.fv, bw),
                  row(3 * LANES), row(3 * LANES),
                  pl.BlockSpec((None, 1, LANES), lambda i: (layer, 0, 0))],
        out_specs=[row(bw), row(bw), row(bw), row(bw), row(lay.iqw), row(LANES), row(LANES), row(LANES),
                   row(bw), row(bw), row(bw)],
        out_shape=[jax.ShapeDtypeStruct((t, bw), BF16), jax.ShapeDtypeStruct((t, bw), F32),
                   jax.ShapeDtypeStruct((t, bw), BF16), jax.ShapeDtypeStruct((t, bw), BF16),
                   jax.ShapeDtypeStruct((t, lay.iqw), BF16), jax.ShapeDtypeStruct((t, LANES), F32),
                   jax.ShapeDtypeStruct((t, LANES), BF16), jax.ShapeDtypeStruct((t, LANES), F32),
                   jax.ShapeDtypeStruct((t, bw), BF16), jax.ShapeDtypeStruct((t, bw), BF16),
                   jax.ShapeDtypeStruct((t, bw), BF16)],
        compiler_params=_params(("parallel",)),
        name="post_proj",
    )(z, z, z, z, z, z, z, z, z, tab_a, tab_i, fb_row)


def _mix_kernel(b0_ref, b1_ref, b2_ref, b3_ref, g0_ref, g1_ref, g2_ref, g3_ref,
                wbr_ref, wout_ref, x_ref, o_ref, acc_ref):
    dt = pl.program_id(1)

    @pl.when(dt == 0)
    def _():
        acc_ref[...] = jnp.zeros_like(acc_ref)

    m = None
    for n, (b_ref, g_ref) in enumerate(((b0_ref, g0_ref), (b1_ref, g1_ref), (b2_ref, g2_ref), (b3_ref, g3_ref))):
        term = jax.nn.sigmoid(g_ref[...]) * _dot(b_ref[...], wbr_ref[n])
        m = term if m is None else m + term
    acc_ref[...] += _dot(m.astype(BF16), wout_ref[...])

    @pl.when(dt == pl.num_programs(1) - 1)
    def _():
        o_ref[...] = x_ref[...] + acc_ref[...]


def _mix(branches, z, w_br, w_out, x, lay, layer):
    t, d = x.shape
    bw = lay.bw
    tm = _pick(t, (640, 512, 256, 128))
    td = 512
    assert d % td == 0 and lay.gate % td == 0
    gspec = lambda n: pl.BlockSpec((tm, td), lambda i, j: (i, (lay.gate + n * d) // td + j))
    bspec = pl.BlockSpec((tm, bw), lambda i, j: (i, 0))
    return pl.pallas_call(
        _mix_kernel,
        grid=(t // tm, d // td),
        in_specs=[bspec, bspec, bspec, bspec, gspec(0), gspec(1), gspec(2), gspec(3),
                  pl.BlockSpec((None, 4, bw, td), lambda i, j: (layer, 0, 0, j)),
                  pl.BlockSpec((None, td, d), lambda i, j: (layer, j, 0)),
                  pl.BlockSpec((tm, d), lambda i, j: (i, 0))],
        out_specs=pl.BlockSpec((tm, d), lambda i, j: (i, 0)),
        out_shape=jax.ShapeDtypeStruct((t, d), F32),
        scratch_shapes=[pltpu.VMEM((tm, d), F32)],
        compiler_params=_params(("parallel", "arbitrary")),
        name="branch_mix",
    )(*branches, z, z, z, z, w_br, w_out, x)


def _ffn_kernel(x_ref, g_ref, wu_ref, wd_ref, o_ref, xn_ref, acc_ref):
    f = pl.program_id(1)

    @pl.when(f == 0)
    def _():
        xn_ref[...] = _rms(x_ref[...], g_ref[...]).astype(BF16)
        acc_ref[...] = jnp.zeros_like(acc_ref)

    h = jnp.maximum(_dot(xn_ref[...], wu_ref[...]), 0.0)
    acc_ref[...] += _dot((h * h).astype(BF16), wd_ref[...])

    @pl.when(f == pl.num_programs(1) - 1)
    def _():
        o_ref[...] = x_ref[...] + acc_ref[...]


def _ffn(x, g, w_up, w_down, layer):
    t, d = x.shape
    dff = w_up.shape[-1]
    tm = _pick(t, (640, 512, 256, 128))
    tf = _pick(dff, (512, 256, 128))
    return pl.pallas_call(
        _ffn_kernel,
        grid=(t // tm, dff // tf),
        in_specs=[pl.BlockSpec((tm, d), lambda i, j: (i, 0)),
                  pl.BlockSpec((None, 1, d), lambda i, j: (layer, 0, 0)),
                  pl.BlockSpec((None, d, tf), lambda i, j: (layer, 0, j)),
                  pl.BlockSpec((None, tf, d), lambda i, j: (layer, j, 0))],
        out_specs=pl.BlockSpec((tm, d), lambda i, j: (i, 0)),
        out_shape=jax.ShapeDtypeStruct((t, d), F32),
        scratch_shapes=[pltpu.VMEM((tm, d), BF16), pltpu.VMEM((tm, d), F32)],
        compiler_params=_params(("parallel", "arbitrary")),
        name="ffn",
    )(x, g, w_up, w_down)


def _final_norm_kernel(x_ref, g_ref, o_ref):
    o_ref[...] = _rms(x_ref[...], g_ref[...])


def _final_norm(x, g):
    t, d = x.shape
    tm = _pick(t, (640, 512, 256, 128))
    return pl.pallas_call(
        _final_norm_kernel,
        grid=(t // tm,),
        in_specs=[pl.BlockSpec((tm, d), lambda i: (i, 0)), pl.BlockSpec((1, d), lambda i: (0, 0))],
        out_specs=pl.BlockSpec((tm, d), lambda i: (i, 0)),
        out_shape=jax.ShapeDtypeStruct((t, d), F32),
        compiler_params=_params(("parallel",)),
        name="final_norm",
    )(x, g)


def _conv_prompt_kernel(ch_ref, cb_ref, cc_ref, w_ref, y_ref, last_ref, carry_ref):
    @pl.when(pl.program_id(0) == 0)
    def _():
        carry_ref[...] = jnp.zeros_like(carry_ref)

    u = cc_ref[...] * ch_ref[...]
    tm = u.shape[0]
    rid = lax.broadcasted_iota(I32, u.shape, 0)
    prev = carry_ref[...]
    acc = w_ref[CONV_W - 1:CONV_W, :] * u
    for back in range(1, CONV_W):
        head = jnp.concatenate([pltpu.roll(prev, back, 0), jnp.zeros((tm - 8, u.shape[1]), F32)], axis=0)
        shifted = jnp.where(rid < back, head, pltpu.roll(u, back, 0))
        acc = acc + w_ref[CONV_W - 1 - back:CONV_W - back, :] * shifted
    y_ref[...] = (cb_ref[...] * acc).astype(y_ref.dtype)
    carry_ref[...] = u[tm - 8:tm, :]
    last_ref[...] = u[tm - 8:tm, :]


def _conv_prompt(z, conv_w, tp, lay, layer):
    bw = lay.bw
    tm = _pick(tp, (512, 256, 128))
    col = lambda off: pl.BlockSpec((tm, bw), lambda i: (i, off // bw))
    return pl.pallas_call(
        _conv_prompt_kernel,
        grid=(tp // tm,),
        in_specs=[col(lay.ch), col(lay.cb), col(lay.cc),
                  pl.BlockSpec((None, CONV_W, bw), lambda i: (layer, 0, 0))],
        out_specs=[pl.BlockSpec((tm, bw), lambda i: (i, 0)), pl.BlockSpec((8, bw), lambda i: (0, 0))],
        out_shape=[jax.ShapeDtypeStruct((tp, bw), BF16), jax.ShapeDtypeStruct((8, bw), F32)],
        scratch_shapes=[pltpu.VMEM((8, bw), F32)],
        compiler_params=_params(("arbitrary",)),
        name="conv_prompt",
    )(z, z, z, conv_w)


def _conv_sample_kernel(ch_ref, cb_ref, cc_ref, st_ref, w_ref, y_ref, ns_ref, *, ls, bw):
    u_ext = jnp.concatenate([st_ref[...], cc_ref[...] * ch_ref[...]], axis=1)
    cb = cb_ref[...]
    ys = []
    for t in range(ls):
        acc = None
        for j in range(CONV_W):
            term = w_ref[j:j + 1, :] * u_ext[:, (t + j) * bw:(t + j + 1) * bw]
            acc = term if acc is None else acc + term
        ys.append(cb[:, t * bw:(t + 1) * bw] * acc)
    y_ref[...] = jnp.concatenate(ys, axis=1).astype(y_ref.dtype)
    ns_ref[...] = u_ext[:, ls * bw:(ls + CONV_W - 1) * bw]


def _conv_sample(ch, cb, cc, state, conv_w, layer, bs, ls, bw):
    kern = functools.partial(_conv_sample_kernel, ls=ls, bw=bw)
    full = lambda shape: pl.BlockSpec(shape, lambda i: (0,) * len(shape))
    return pl.pallas_call(
        kern,
        grid=(1,),
        in_specs=[full((bs, ls * bw)), full((bs, ls * bw)), full((bs, ls * bw)),
                  pl.BlockSpec((None, bs, (CONV_W - 1) * bw), lambda i: (layer, 0, 0)),
                  pl.BlockSpec((None, CONV_W, bw), lambda i: (layer, 0, 0))],
        out_specs=[full((bs, ls * bw)), full((bs, (CONV_W - 1) * bw))],
        out_shape=[jax.ShapeDtypeStruct((bs, ls * bw), BF16),
                   jax.ShapeDtypeStruct((bs, (CONV_W - 1) * bw), F32)],
        compiler_params=_params(("arbitrary",)),
        name="conv_sample",
    )(ch, cb, cc, state, conv_w)


def _f32_key(x):
    bits = lax.bitcast_convert_type(x, I32)
    return jnp.where(bits < 0, bits ^ jnp.int32(0x7FFFFFFF), bits)


def _topk_select(key, col, tau, jmax):
    return (key > tau) | ((key == tau) & (col <= jmax))


def _topk_threshold(get_chunk, nch, k, n_valid, rows, width, jmax_ref):
    def count(pred):
        def body(c, part):
            m = jnp.where(pred(c, get_chunk(c)), 1.0, 0.0)
            blocks = [m[:, b * LANES:(b + 1) * LANES] for b in range(width // LANES)]
            while len(blocks) > 1:
                blocks = [a + b for a, b in zip(blocks[0::2], blocks[1::2])] + blocks[len(blocks) & ~1:]
            return part + blocks[0]

        part = lax.fori_loop(0, nch, body, jnp.zeros((rows, LANES), F32))
        return jnp.sum(part, axis=1, keepdims=True)

    kf = float(k)
    take_all = n_valid <= kf
    cnt0 = count(lambda c, kb: kb >= 0)
    base0 = jnp.where(cnt0 >= kf, jnp.int32(0), jnp.int32(INT_MIN))
    done0 = jnp.where(take_all | (cnt0 == kf), 1, 0).astype(I32)

    def left(done):
        return jnp.sum(jnp.where(done > 0, 0.0, 1.0))

    def cond(st):
        it, _, _, nleft = st
        return (it < 31) & (nleft > 0.0)

    def body(st):
        it, base, done, _ = st
        for u in range(BISECT_STEPS_PER_CHECK):
            bit = 30 - it - u
            cand = base | jnp.where(bit >= 0, jnp.left_shift(jnp.int32(1), jnp.maximum(bit, 0)), 0)
            cnt = count(lambda c, kb, cand=cand: kb >= cand)
            base = jnp.where(cnt >= kf, cand, base)
            done = jnp.where(cnt == kf, 1, done)
        return it + BISECT_STEPS_PER_CHECK, base, done, left(done)

    _, base, done, nleft = lax.while_loop(cond, body, (jnp.int32(0), base0, done0, left(done0)))
    tau = jnp.where(take_all, jnp.int32(INT_MIN), base)
    jmax_ref[...] = jnp.full(jmax_ref.shape, 2 ** 31 - 1, I32)

    @pl.when(nleft > 0.0)
    def _():
        lane = lax.broadcasted_iota(I32, (rows, width), 1)
        need = kf - count(lambda c, kb: kb > tau)

        def idx_body(it, lo):
            cand = lo | jnp.left_shift(jnp.int32(1), 30 - it)
            cnt = count(lambda c, kb: (kb == tau) & (c * width + lane < cand))
            return jnp.where(cnt < need, cand, lo)

        lo = lax.fori_loop(0, 31, idx_body, jnp.zeros((rows, 1), I32))
        jmax = jnp.where(done > 0, jnp.int32(2 ** 31 - 1), lo)
        jmax_ref[...] = jnp.broadcast_to(jmax, jmax_ref.shape)

    return tau, jmax_ref[:, 0:1]


def _dsa_prompt_kernel(iq_ref, mw_ref, aq_ref, ik_ref, ak_ref, av_ref, o_ref, sc_ref, jm_ref,
                       *, tq, tk, n_sel, nh, scale):
    i = pl.program_id(0)
    nch = ((i + 1) * tq + tk - 1) // tk
    row = i * tq + lax.broadcasted_iota(I32, (tq, tk), 0)
    col0 = lax.broadcasted_iota(I32, (tq, tk), 1)
    w = mw_ref[...]

    def score_body(c, carry):
        off = pl.multiple_of(c * tk, tk)
        ikc = ik_ref[pl.ds(off, tk), :]
        acc = jnp.zeros((tq, tk), F32)
        for h in range(IDX_HEADS):
            d = _dot_nt(iq_ref[:, h * LANES:(h + 1) * LANES], ikc)
            acc = acc + w[:, h:h + 1] * jnp.maximum(d, 0.0)
        s = jnp.where(off + col0 <= row, acc + 0.0, NEG_BIG)
        sc_ref[c] = _f32_key(s)
        return carry

    lax.fori_loop(0, nch, score_body, 0)
    n_valid = (row[:, 0:1] + 1).astype(F32)
    tau, jmax = _topk_threshold(lambda c: sc_ref[c], nch, n_sel, n_valid, tq, tk, jm_ref)

    def att_body(c, carry):
        off = pl.multiple_of(c * tk, tk)
        col = off + col0
        bias = jnp.where(_topk_select(sc_ref[c], col, tau, jmax) & (col <= row), 0.0, NEG_BIG)
        new = []
        for h in range(nh):
            m, l, acc = carry[h]
            hs = slice(h * HEAD_DIM, (h + 1) * HEAD_DIM)
            s = _dot_nt(aq_ref[:, hs], ak_ref[pl.ds(off, tk), hs]) * scale + bias
            m_new = jnp.maximum(m, jnp.max(s, axis=1, keepdims=True))
            p = jnp.exp(s - m_new)
            alpha = jnp.exp(m - m_new)
            l = l * alpha + jnp.sum(p, axis=1, keepdims=True)
            acc = acc * alpha + _dot(p.astype(BF16), av_ref[pl.ds(off, tk), hs])
            new.append((m_new, l, acc))
        return tuple(new)

    init = tuple((jnp.full((tq, 1), NEG_BIG, F32), jnp.zeros((tq, 1), F32), jnp.zeros((tq, HEAD_DIM), F32))
                 for _ in range(nh))
    fin = lax.fori_loop(0, nch, att_body, init)
    o_ref[...] = jnp.concatenate([acc / l for _, l, acc in fin], axis=1).astype(o_ref.dtype)


def _dsa_prompt(iq_r, misc2, aq_r, ik_b, ak_b, av_b, tp, lay):
    bw = lay.bw
    tq = 256
    tk = _pick(tp, (512, 256, 128))
    n_sel = min(TOPK_MAX, tp // 4)
    kern = functools.partial(_dsa_prompt_kernel, tq=tq, tk=tk, n_sel=n_sel, nh=lay.nh,
                             scale=float(HEAD_DIM ** -0.5))
    res = lambda w: pl.BlockSpec((tp, w), lambda i: (0, 0))
    return pl.pallas_call(
        kern,
        grid=(tp // tq,),
        in_specs=[pl.BlockSpec((tq, lay.iqw), lambda i: (i, 0)),
                  pl.BlockSpec((tq, LANES), lambda i: (i, 0)),
                  pl.BlockSpec((tq, bw), lambda i: (i, 0)),
                  res(LANES), res(bw), res(bw)],
        out_specs=pl.BlockSpec((tq, bw), lambda i: (i, 0)),
        out_shape=jax.ShapeDtypeStruct((tp, bw), BF16),
        scratch_shapes=[pltpu.VMEM((tp // tk, tq, tk), I32), pltpu.VMEM((tq, LANES), I32)],
        compiler_params=_params(("arbitrary",)),
        name="dsa_prompt",
    )(iq_r, misc2, aq_r, ik_b, ak_b, av_b)


def _new_key_valid(shape, ls):
    t = lax.broadcasted_iota(I32, shape, 0) & (SAMPLE_ROWS - 1)
    c = lax.broadcasted_iota(I32, shape, 1)
    return (c <= t) & (c < ls)


def _page_spec(shape, layer, g, gi):
    return pl.BlockSpec((None, None) + shape, lambda b, j, pt: (layer, pt[b, j * g + gi], 0, 0))


def _dsa_sample_score_kernel(pt_ref, iq_ref, iw_ref, *rest, npg, g, page, ls):
    pool_refs = rest[:g]
    new_ref, key_ref, keyn_ref = rest[g:]
    j = pl.program_id(1)

    def index_keys(keys_t, valid):
        d = _dot(iq_ref[:, 0:IDX_DIM], keys_t.astype(BF16))
        r = jnp.maximum(d, 0.0) * iw_ref[:, 0:1]
        s = r[0:SAMPLE_ROWS, :]
        for h in range(1, IDX_HEADS):
            s = s + r[h * SAMPLE_ROWS:(h + 1) * SAMPLE_ROWS, :]
        s = s + 0.0
        return _f32_key(s if valid is None else jnp.where(valid, s, NEG_BIG))

    key_ref[...] = index_keys(jnp.concatenate([r[...] for r in pool_refs], axis=1), None)

    @pl.when(j == npg // g - 1)
    def _():
        keyn_ref[...] = index_keys(new_ref[...], _new_key_valid((SAMPLE_ROWS, page), ls))


def _dsa_sample_score(page_table, iq_s, iw_s, pool, ik_new, layer, ls):
    bs, npg = page_table.shape
    page = pool.shape[3]
    g = _pick(npg, (8, 4, 2, 1))
    kern = functools.partial(_dsa_sample_score_kernel, npg=npg, g=g, page=page, ls=ls)
    per_seq = lambda shape: pl.BlockSpec((None,) + shape, lambda b, j, pt: (b, 0, 0))
    grid_spec = pltpu.PrefetchScalarGridSpec(
        num_scalar_prefetch=1,
        grid=(bs, npg // g),
        in_specs=[per_seq((IDX_HEADS * SAMPLE_ROWS, LANES)), per_seq((IDX_HEADS * SAMPLE_ROWS, LANES))]
        + [_page_spec((IDX_DIM, page), layer, g, gi) for gi in range(g)]
        + [per_seq((IDX_DIM, page))],
        out_specs=[pl.BlockSpec((None, SAMPLE_ROWS, g * page), lambda b, j, pt: (b, 0, j)),
                   per_seq((SAMPLE_ROWS, page))],
    )
    return pl.pallas_call(
        kern,
        grid_spec=grid_spec,
        out_shape=[jax.ShapeDtypeStruct((bs, SAMPLE_ROWS, npg * page), I32),
                   jax.ShapeDtypeStruct((bs, SAMPLE_ROWS, page), I32)],
        compiler_params=_params(("parallel", "arbitrary")),
        name="dsa_sample_score",
    )(page_table, iq_s, iw_s, *([pool] * g), ik_new)


def _dsa_sample_thresh_kernel(key_ref, keyn_ref, tau_ref, sc_ref, jm_ref, *, nseq, past, ls, n_sel):
    rows = nseq * SAMPLE_ROWS
    width = sc_ref.shape[2]
    sc_ref[0] = jnp.concatenate([key_ref[...].reshape(rows, key_ref.shape[2]),
                                 keyn_ref[...].reshape(rows, keyn_ref.shape[2])], axis=1)
    t = lax.broadcasted_iota(I32, (rows, 1), 0) & (SAMPLE_ROWS - 1)
    n_valid = (past + jnp.minimum(t + 1, ls)).astype(F32)
    tau, jmax = _topk_threshold(lambda c: sc_ref[c], 1, n_sel, n_valid, rows, width, jm_ref)
    lane = lax.broadcasted_iota(I32, (rows, LANES), 1)
    tau_ref[...] = jnp.where(lane == 0, tau, jmax).reshape(tau_ref.shape)


def _dsa_sample_thresh(keys, keys_new, ls):
    bs, _, past = keys.shape
    page = keys_new.shape[2]
    nseq = _pick(bs, (4, 2, 1))
    n_sel = min(TOPK_MAX, (past + ls) // 4)
    kern = functools.partial(_dsa_sample_thresh_kernel, nseq=nseq, past=past, ls=ls, n_sel=n_sel)
    blk = lambda w: pl.BlockSpec((nseq, SAMPLE_ROWS, w), lambda b: (b, 0, 0))
    return pl.pallas_call(
        kern,
        grid=(bs // nseq,),
        in_specs=[blk(past), blk(page)],
        out_specs=blk(LANES),
        out_shape=jax.ShapeDtypeStruct((bs, SAMPLE_ROWS, LANES), I32),
        scratch_shapes=[pltpu.VMEM((1, nseq * SAMPLE_ROWS, past + page), I32),
                        pltpu.VMEM((nseq * SAMPLE_ROWS, LANES), I32)],
        compiler_params=_params(("parallel",)),
        name="dsa_sample_thresh",
    )(keys, keys_new)


def _stack_heads(q, nh):
    return jnp.concatenate([q[:, h * HEAD_DIM:(h + 1) * HEAD_DIM] for h in range(nh)], axis=0)


def _unstack_heads(o, nh):
    return jnp.concatenate([o[h * SAMPLE_ROWS:(h + 1) * SAMPLE_ROWS, :] for h in range(nh)], axis=1)


def _page_patterns(page, nh):
    shift = nh.bit_length() - 1
    assert 1 << shift == nh
    shape = (nh * SAMPLE_ROWS, page * nh)
    r = lax.broadcasted_iota(I32, shape, 0)
    c = lax.broadcasted_iota(I32, shape, 1)
    own_head = (c & (nh - 1)) == (r >> 3)
    new_valid = lambda ls: ((c >> shift) <= (r & (SAMPLE_ROWS - 1))) & ((c >> shift) < ls)
    er = lax.broadcasted_iota(I32, (page, page * nh), 0)
    ec = lax.broadcasted_iota(I32, (page, page * nh), 1)
    expand = jnp.where((ec >> shift) == er, 1.0, 0.0)
    return own_head, new_valid, expand


def _online_softmax_step(logits, masks, values, m_ref, l_ref, acc_ref):
    m = m_ref[...]
    m_new = m
    for s in logits:
        m_new = jnp.maximum(m_new, jnp.max(s, axis=1, keepdims=True))
    alpha = jnp.exp(m - m_new)
    l = l_ref[...] * alpha
    acc = acc_ref[...] * alpha
    for s, msk, v in zip(logits, masks, values):
        p = jnp.where(msk, jnp.exp(s - m_new), 0.0)
        l = l + jnp.sum(p, axis=1, keepdims=True)
        acc = acc + _dot(p.astype(BF16), v)
    l_ref[...] = l
    acc_ref[...] = acc
    m_ref[...] = m_new


def _sample_attn_init(q_ref, q2_ref, m_ref, l_ref, acc_ref, nh):
    q2_ref[...] = _stack_heads(q_ref[...], nh).astype(BF16)
    m_ref[...] = jnp.full_like(m_ref, NEG_BIG)
    l_ref[...] = jnp.zeros_like(l_ref)
    acc_ref[...] = jnp.zeros_like(acc_ref)


def _sample_attn_scratch(nh):
    rows = nh * SAMPLE_ROWS
    return [pltpu.VMEM((rows, HEAD_DIM), BF16), pltpu.VMEM((rows, 1), F32),
            pltpu.VMEM((rows, 1), F32), pltpu.VMEM((rows, HEAD_DIM), F32)]


def _dsa_sample_attn_kernel(pt_ref, q_ref, key_ref, keyn_ref, tau_ref, *rest, npg, g, page, ls, nh, scale):
    kp_refs, vp_refs = rest[:g], rest[g:2 * g]
    kn_ref, vn_ref, o_ref, q2_ref, m_ref, l_ref, acc_ref = rest[2 * g:]
    j = pl.program_id(1)

    @pl.when(j == 0)
    def _():
        _sample_attn_init(q_ref, q2_ref, m_ref, l_ref, acc_ref, nh)

    own_head, new_valid, expand = _page_patterns(page, nh)
    expand = expand.astype(BF16)
    tau, jmax = tau_ref[:, 0:1], tau_ref[:, 1:2]
    lane = lax.broadcasted_iota(I32, (SAMPLE_ROWS, page), 1)

    def attend(kps, vps, key8s, col0s, valid):
        sel8 = jnp.concatenate([jnp.where(_topk_select(k8, c0 + lane, tau, jmax), 1.0, 0.0)
                                for k8, c0 in zip(key8s, col0s)], axis=0).astype(BF16)
        sel_x = _dot(sel8, expand) > 0.5
        q2 = q2_ref[...]
        logits, masks = [], []
        for n, kp in enumerate(kps):
            sel = jnp.concatenate([sel_x[n * SAMPLE_ROWS:(n + 1) * SAMPLE_ROWS, :]] * nh, axis=0)
            msk = sel & own_head if valid is None else sel & own_head & valid
            logits.append(jnp.where(msk, _dot_nt(q2, kp.astype(BF16)) * scale, NEG_BIG))
            masks.append(msk)
        _online_softmax_step(logits, masks, [vp.astype(BF16) for vp in vps], m_ref, l_ref, acc_ref)

    attend([r[...] for r in kp_refs], [r[...] for r in vp_refs],
           [key_ref[:, gi * page:(gi + 1) * page] for gi in range(g)],
           [(j * g + gi) * page for gi in range(g)], None)

    @pl.when(j == npg // g - 1)
    def _():
        attend([kn_ref[...]], [vn_ref[...]], [keyn_ref[...]], [npg * page], new_valid(ls))
        o_ref[...] = _unstack_heads(acc_ref[...] / l_ref[...], nh).astype(o_ref.dtype)


def _dsa_sample_attn(page_table, q_s, keys, keys_new, tau, kpool, vpool, k_new, v_new, layer, ls, lay):
    bs, npg = page_table.shape
    rows = kpool.shape[2]
    bw, nh = lay.bw, lay.nh
    page = rows // nh
    g = _pick(npg, ATTN_PAGES_PER_STEP)
    kern = functools.partial(_dsa_sample_attn_kernel, npg=npg, g=g, page=page, ls=ls, nh=nh,
                             scale=float(HEAD_DIM ** -0.5))
    per_seq = lambda shape: pl.BlockSpec((None,) + shape, lambda b, j, pt: (b, 0, 0))
    pages = [_page_spec((rows, HEAD_DIM), layer, g, gi) for gi in range(g)]
    grid_spec = pltpu.PrefetchScalarGridSpec(
        num_scalar_prefetch=1,
        grid=(bs, npg // g),
        in_specs=[per_seq((SAMPLE_ROWS, bw)),
                  pl.BlockSpec((None, SAMPLE_ROWS, g * page), lambda b, j, pt: (b, 0, j)),
                  per_seq((SAMPLE_ROWS, page)), per_seq((SAMPLE_ROWS, LANES))]
        + pages + pages + [per_seq((rows, HEAD_DIM)), per_seq((rows, HEAD_DIM))],
        out_specs=per_seq((SAMPLE_ROWS, bw)),
        scratch_shapes=_sample_attn_scratch(nh),
    )
    return pl.pallas_call(
        kern,
        grid_spec=grid_spec,
        out_shape=jax.ShapeDtypeStruct((bs, SAMPLE_ROWS, bw), BF16),
        compiler_params=_params(("parallel", "arbitrary")),
        name="dsa_sample_attn",
    )(page_table, q_s, keys, keys_new, tau, *([kpool] * g), *([vpool] * g), k_new, v_new)


def _upper_ones(n):
    r = lax.broadcasted_iota(I32, (n, n), 0)
    c = lax.broadcasted_iota(I32, (n, n), 1)
    return jnp.where(r <= c, 1.0, 0.0).astype(F32)


def _fox_cumsum_kernel(m2_ref, crow_ref, ccol_ref, carry_ref):
    @pl.when(pl.program_id(0) == 0)
    def _():
        carry_ref[...] = jnp.zeros_like(carry_ref)

    lf_t = m2_ref[...].T
    tc = lf_t.shape[1]
    cs = _dot(lf_t, _upper_ones(tc), HIGHEST) + carry_ref[:, 0:1]
    crow_ref[...] = cs
    ccol_ref[...] = cs.T
    carry_ref[...] = jnp.broadcast_to(cs[:, tc - 1:tc], carry_ref.shape)


def _fox_cumsum(misc2, tp):
    tc = _pick(tp, (512, 256, 128))
    return pl.pallas_call(
        _fox_cumsum_kernel,
        grid=(tp // tc,),
        in_specs=[pl.BlockSpec((tc, LANES), lambda i: (i, 0))],
        out_specs=[pl.BlockSpec((LANES, tc), lambda i: (0, i)), pl.BlockSpec((tc, LANES), lambda i: (i, 0))],
        out_shape=[jax.ShapeDtypeStruct((LANES, tp), F32), jax.ShapeDtypeStruct((tp, LANES), F32)],
        scratch_shapes=[pltpu.VMEM((LANES, LANES), F32)],
        compiler_params=_params(("arbitrary",)),
        name="fox_cumsum",
    )(misc2)


def _fox_prompt_kernel(q_ref, k_ref, v_ref, ccol_ref, crow_ref, o_ref, m_ref, l_ref, acc_ref,
                       *, tq, tk, nh, ff_lane, scale):
    i, j = pl.program_id(0), pl.program_id(1)

    @pl.when(j == 0)
    def _():
        m_ref[...] = jnp.full_like(m_ref, NEG_BIG)
        l_ref[...] = jnp.zeros_like(l_ref)
        acc_ref[...] = jnp.zeros_like(acc_ref)

    def block(diagonal):
        def sub(r, carry):
            rows = pl.ds(pl.multiple_of(r * sub_q, sub_q), sub_q)
            if diagonal:
                below = lax.broadcasted_iota(I32, (sub_q, tk), 1) <= r * sub_q + lax.broadcasted_iota(I32, (sub_q, tk), 0)
            for h in range(nh):
                hs = slice(h * HEAD_DIM, (h + 1) * HEAD_DIM)
                s = _dot_nt(q_ref[rows, hs], k_ref[:, hs]) * scale
                s = s + ccol_ref[rows, ff_lane + h:ff_lane + h + 1] - crow_ref[ff_lane + h:ff_lane + h + 1, :]
                if diagonal:
                    s = jnp.where(below, s, NEG_BIG)
                m = m_ref[h, rows, :]
                m_new = jnp.maximum(m, jnp.max(s, axis=1, keepdims=True))
                p = jnp.exp(s - m_new[:, 0:1])
                alpha = jnp.exp(m - m_new)
                l_ref[h, rows, :] = l_ref[h, rows, :] * alpha + jnp.sum(p, axis=1, keepdims=True)
                acc_ref[h, rows, :] = acc_ref[h, rows, :] * alpha + _dot(p.astype(BF16), v_ref[:, hs])
                m_ref[h, rows, :] = m_new
            return carry

        lax.fori_loop(0, tq // sub_q, sub, 0)

    sub_q = min(tq, 256)

    @pl.when(j < i)
    def _():
        block(False)

    @pl.when(j == i)
    def _():
        block(True)
        o_ref[...] = jnp.concatenate([acc_ref[h] / l_ref[h] for h in range(nh)], axis=1).astype(o_ref.dtype)


def _fox_prompt(fq_b, fk_b, fv_b, ccol, crow, tp, lay):
    bw = lay.bw
    tq = tk = _pick(tp, (512, 256, 128))
    kern = functools.partial(_fox_prompt_kernel, tq=tq, tk=tk, nh=lay.nh, ff_lane=lay.ff_lane,
                             scale=float(HEAD_DIM ** -0.5))
    return pl.pallas_call(
        kern,
        grid=(tp // tq, tp // tk),
        in_specs=[pl.BlockSpec((tq, bw), lambda i, j: (i, 0)),
                  pl.BlockSpec((tk, bw), lambda i, j: (jnp.minimum(j, i), 0)),
                  pl.BlockSpec((tk, bw), lambda i, j: (jnp.minimum(j, i), 0)),
                  pl.BlockSpec((tq, LANES), lambda i, j: (i, 0)),
                  pl.BlockSpec((LANES, tk), lambda i, j: (0, jnp.minimum(j, i)))],
        out_specs=pl.BlockSpec((tq, bw), lambda i, j: (i, 0)),
        out_shape=jax.ShapeDtypeStruct((tp, bw), BF16),
        scratch_shapes=[pltpu.VMEM((lay.nh, tq, LANES), F32), pltpu.VMEM((lay.nh, tq, LANES), F32),
                        pltpu.VMEM((lay.nh, tq, HEAD_DIM), F32)],
        compiler_params=_params(("parallel", "arbitrary")),
        name="fox_prompt",
    )(fq_b, fk_b, fv_b, ccol, crow)


def _fox_sample_kernel(pt_ref, q_ref, *rest, npg, g, page, ls, nh, scale):
    kp_refs, vp_refs, lf_refs = rest[:g], rest[g:2 * g], rest[2 * g:3 * g]
    kn_ref, vn_ref, lfn_ref, o_ref, carry_ref, q2_ref, m_ref, l_ref, acc_ref = rest[3 * g:]
    j = pl.program_id(1)

    @pl.when(j == 0)
    def _():
        _sample_attn_init(q_ref, q2_ref, m_ref, l_ref, acc_ref, nh)
        carry_ref[...] = jnp.zeros_like(carry_ref)

    own_head, new_valid, expand = _page_patterns(page, nh)
    upper = _upper_ones(page)

    def attend(kps, vps, lfs, valid):
        within = _dot(jnp.concatenate(lfs, axis=0), upper, HIGHEST)
        carry = carry_ref[:, 0:1]
        cs = []
        for n in range(len(lfs)):
            w = within[n * SAMPLE_ROWS:(n + 1) * SAMPLE_ROWS, :]
            cs.append(w + carry)
            carry = carry + w[:, page - 1:page]
        carry_ref[...] = jnp.broadcast_to(carry, carry_ref.shape)
        cs_x = _dot(jnp.concatenate(cs, axis=0), expand, HIGHEST)
        q2 = q2_ref[...]
        msk = own_head if valid is None else own_head & valid
        logits = []
        for n, kp in enumerate(kps):
            bias = jnp.concatenate([jnp.broadcast_to(cs_x[n * SAMPLE_ROWS + h:n * SAMPLE_ROWS + h + 1, :],
                                                     (SAMPLE_ROWS, page * nh)) for h in range(nh)], axis=0)
            logits.append(jnp.where(msk, _dot_nt(q2, kp.astype(BF16)) * scale - bias, NEG_BIG))
        _online_softmax_step(logits, [msk] * len(kps), [vp.astype(BF16) for vp in vps], m_ref, l_ref, acc_ref)

    attend([r[...] for r in kp_refs], [r[...] for r in vp_refs], [r[...] for r in lf_refs], None)

    @pl.when(j == npg // g - 1)
    def _():
        attend([kn_ref[...]], [vn_ref[...]], [lfn_ref[...]], new_valid(ls))
        o_ref[...] = _unstack_heads(acc_ref[...] / l_ref[...], nh).astype(o_ref.dtype)


def _fox_sample(page_table, q_s, kpool, vpool, lfpool_t, k_new, v_new, lf_new, layer, ls, lay):
    bs, npg = page_table.shape
    rows = kpool.shape[2]
    bw, nh = lay.bw, lay.nh
    page = rows // nh
    g = _pick(npg, ATTN_PAGES_PER_STEP)
    kern = functools.partial(_fox_sample_kernel, npg=npg, g=g, page=page, ls=ls, nh=nh,
                             scale=float(HEAD_DIM ** -0.5))
    per_seq = lambda shape: pl.BlockSpec((None,) + shape, lambda b, j, pt: (b, 0, 0))
    pages = [_page_spec((rows, HEAD_DIM), layer, g, gi) for gi in range(g)]
    grid_spec = pltpu.PrefetchScalarGridSpec(
        num_scalar_prefetch=1,
        grid=(bs, npg // g),
        in_specs=[per_seq((SAMPLE_ROWS, bw))] + pages + pages
        + [_page_spec((SAMPLE_ROWS, page), layer, g, gi) for gi in range(g)]
        + [per_seq((rows, HEAD_DIM)), per_seq((rows, HEAD_DIM)), per_seq((SAMPLE_ROWS, page))],
        out_specs=per_seq((SAMPLE_ROWS, bw)),
        scratch_shapes=[pltpu.VMEM((SAMPLE_ROWS, LANES), F32)] + _sample_attn_scratch(nh),
    )
    return pl.pallas_call(
        kern,
        grid_spec=grid_spec,
        out_shape=jax.ShapeDtypeStruct((bs, SAMPLE_ROWS, bw), BF16),
        compiler_params=_params(("parallel", "arbitrary")),
        name="fox_sample",
    )(page_table, q_s, *([kpool] * g), *([vpool] * g), *([lfpool_t] * g), k_new, v_new, lf_new)


def _hgrn_lower_bound(lg_ref, layer):
    lg = lg_ref[...]
    e = jnp.exp(lg - jnp.max(lg, axis=0, keepdims=True))
    p = e / jnp.sum(e, axis=0, keepdims=True)
    lb = jnp.zeros((1, lg.shape[1]), F32)
    for i in range(1, layer + 1):
        lb = lb + p[i:i + 1, :]
    return lb


def _hgrn_rows(hq, hf, hi, lb, chunk, row_valid, nh):
    r = hq.shape[0]
    pos = lax.broadcasted_iota(I32, hq.shape, 0) & (chunk - 1)
    lf = jnp.log(lb + (1.0 - lb) * jax.nn.sigmoid(hf))
    kk = (1.0 - lb) * jax.nn.sigmoid(-hf)
    if row_valid is not None:
        lf = jnp.where(row_valid, lf, 0.0)
        kk = jnp.where(row_valid, kk, 0.0)
    b = lf
    step = 1
    while step < chunk:
        b = b + jnp.where(pos >= step, pltpu.roll(b, step, 0), 0.0)
        step *= 2
    suf = lf
    step = 1
    while step < chunk:
        suf = suf + jnp.where(pos + step < chunk, pltpu.roll(suf, r - step, 0), 0.0)
        step *= 2
    tail = suf - lf
    qe = hq * jnp.exp(b)
    kd = kk * jnp.exp(tail)
    etot = jnp.exp(b + tail)
    o_intra = jnp.zeros_like(hq)
    for d in range(chunk):
        if d == 0:
            kk_d, b_d, i_d = kk, b, hi
        else:
            kk_d, b_d, i_d = pltpu.roll(kk, d, 0), pltpu.roll(b, d, 0), pltpu.roll(hi, d, 0)
        dec = jnp.exp(jnp.where(pos >= d, b - b_d, NEG_BIG))
        prod = hq * dec * kk_d
        a = jnp.concatenate(
            [jnp.broadcast_to(jnp.sum(prod[:, h * HEAD_DIM:(h + 1) * HEAD_DIM], axis=1, keepdims=True), (r, HEAD_DIM))
             for h in range(nh)], axis=1)
        o_intra = o_intra + a * i_d
    return qe, kd, etot, o_intra


def _hgrn_finish(o, hg, ng, nh):
    outs = []
    for h in range(nh):
        oh = o[:, h * HEAD_DIM:(h + 1) * HEAD_DIM]
        outs.append(oh * lax.rsqrt(jnp.mean(oh * oh, axis=1, keepdims=True) + EPS))
    return jnp.concatenate(outs, axis=1) * ng * (hg * jax.nn.sigmoid(hg))


def _hgrn_prompt_kernel(hq_ref, hf_ref, hi_ref, hg_ref, lg_ref, ng_ref, y_ref, s_ref,
                        st_ref, qe_ref, kd_ref, et_ref, oi_ref, *, layer, chunk, nh):
    @pl.when(pl.program_id(0) == 0)
    def _():
        st_ref[...] = jnp.zeros_like(st_ref)

    lb = _hgrn_lower_bound(lg_ref, layer)
    hi = hi_ref[...]
    qe, kd, etot, o_intra = _hgrn_rows(hq_ref[...], hf_ref[...], hi, lb, chunk, None, nh)
    qe_ref[...] = qe.astype(BF16)
    kd_ref[...] = kd.astype(BF16)
    et_ref[...] = etot
    tm = hi.shape[0]

    def chunk_body(c, carry):
        rows = pl.ds(pl.multiple_of(c * chunk, chunk), chunk)
        for h in range(nh):
            hs = slice(h * HEAD_DIM, (h + 1) * HEAD_DIM)
            st = st_ref[h]
            oi_ref[rows, hs] = _dot_nt(qe_ref[rows, hs], st.astype(BF16))
            e = et_ref[rows, hs]
            st_ref[h] = st * e[0:1, :] + _dot_tn(hi_ref[rows, hs].astype(BF16), kd_ref[rows, hs])
        return carry

    lax.fori_loop(0, tm // chunk, chunk_body, 0)
    y_ref[...] = _hgrn_finish(oi_ref[...] + o_intra, hg_ref[...], ng_ref[...], nh).astype(y_ref.dtype)

    @pl.when(pl.program_id(0) == pl.num_programs(0) - 1)
    def _():
        for h in range(nh):
            s_ref[h] = st_ref[h].T


def _hgrn_prompt(z, lb_logits, norm_g, tp, lay, layer):
    bw, nh = lay.bw, lay.nh
    tm = _pick(tp, (512, 256, 128))
    dep = lb_logits.shape[0]
    col = lambda off: pl.BlockSpec((tm, bw), lambda i: (i, off // bw))
    kern = functools.partial(_hgrn_prompt_kernel, layer=layer, chunk=HG_CHUNK_PROMPT, nh=nh)
    return pl.pallas_call(
        kern,
        grid=(tp // tm,),
        in_specs=[col(lay.hq), col(lay.hf), col(lay.hi), col(lay.hg),
                  pl.BlockSpec((dep, bw), lambda i: (0, 0)),
                  pl.BlockSpec((None, 1, bw), lambda i: (layer, 0, 0))],
        out_specs=[pl.BlockSpec((tm, bw), lambda i: (i, 0)),
                   pl.BlockSpec((nh, HEAD_DIM, HEAD_DIM), lambda i: (0, 0, 0))],
        out_shape=[jax.ShapeDtypeStruct((tp, bw), BF16), jax.ShapeDtypeStruct((nh, HEAD_DIM, HEAD_DIM), F32)],
        scratch_shapes=[pltpu.VMEM((nh, HEAD_DIM, HEAD_DIM), F32), pltpu.VMEM((tm, bw), BF16),
                        pltpu.VMEM((tm, bw), BF16), pltpu.VMEM((tm, bw), F32), pltpu.VMEM((tm, bw), F32)],
        compiler_params=_params(("arbitrary",)),
        name="hgrn_prompt",
    )(z, z, z, z, lb_logits, norm_g)


def _hgrn_sample_kernel(hq_ref, hf_ref, hi_ref, hg_ref, lg_ref, ng_ref, s0_ref, y_ref, s_ref,
                        *, layer, ls, nh):
    lb = _hgrn_lower_bound(lg_ref, layer)
    hi = hi_ref[...]
    row_valid = lax.broadcasted_iota(I32, hi.shape, 0) < ls
    hi = jnp.where(row_valid, hi, 0.0)
    qe, kd, etot, o_intra = _hgrn_rows(hq_ref[...], hf_ref[...], hi, lb, SAMPLE_ROWS, row_valid, nh)
    outs = []
    for h in range(nh):
        hs = slice(h * HEAD_DIM, (h + 1) * HEAD_DIM)
        st = s0_ref[h].T
        outs.append(_dot_nt(qe[:, hs].astype(BF16), st.astype(BF16)))
        st = st * etot[0:1, hs] + _dot_tn(hi[:, hs].astype(BF16), kd[:, hs].astype(BF16))
        s_ref[h] = st.T
    o = jnp.concatenate(outs, axis=1) + o_intra
    y_ref[...] = _hgrn_finish(o, hg_ref[...], ng_ref[...], nh).astype(y_ref.dtype)


def _hgrn_sample(hq, hf, hi, hg, lb_logits, norm_g, state, layer, ls, lay):
    bs = hq.shape[0]
    bw, nh = lay.bw, lay.nh
    dep = lb_logits.shape[0]
    rows = pl.BlockSpec((None, SAMPLE_ROWS, bw), lambda b: (b, 0, 0))
    kern = functools.partial(_hgrn_sample_kernel, layer=layer, ls=ls, nh=nh)
    return pl.pallas_call(
        kern,
        grid=(bs,),
        in_specs=[rows, rows, rows, rows,
                  pl.BlockSpec((dep, bw), lambda b: (0, 0)),
                  pl.BlockSpec((None, 1, bw), lambda b: (layer, 0, 0)),
                  pl.BlockSpec((None, None, nh, HEAD_DIM, HEAD_DIM), lambda b: (layer, b, 0, 0, 0))],
        out_specs=[rows, pl.BlockSpec((None, nh, HEAD_DIM, HEAD_DIM), lambda b: (b, 0, 0, 0))],
        out_shape=[jax.ShapeDtypeStruct((bs, SAMPLE_ROWS, bw), BF16),
                   jax.ShapeDtypeStruct((bs, nh, HEAD_DIM, HEAD_DIM), F32)],
        compiler_params=_params(("parallel",)),
        name="hgrn_sample",
    )(hq, hf, hi, hg, lb_logits, norm_g, state)


def _pad_rows(a, bs, ls, rows):
    return jnp.pad(a.reshape(bs, ls, a.shape[-1]), ((0, 0), (0, rows - ls), (0, 0)))


def kernel(x_prompt, x_sample, cache_dsa_k, cache_dsa_v, cache_dsa_kidx, cache_fox_k, cache_fox_v,
           cache_fox_logf, state_conv, state_hgrn, page_table, norm_mix_g, w_in, conv_w, fox_fb,
           hgrn_lb_logits, hgrn_norm_g, w_branch, w_out, norm_ffn_g, w_up, w_down, final_norm_g):
    bp, tp, d = x_prompt.shape
    bs, ls, _ = x_sample.shape
    assert bp == 1 and ls <= SAMPLE_ROWS
    depth = w_in.shape[0]
    lay = _Layout(d)
    bw, nh = lay.bw, lay.nh
    ts = bs * ls
    npg = page_table.shape[1]
    n_pool, page = cache_dsa_k.shape[1], cache_dsa_k.shape[2]
    assert page == LANES and tp % LANES == 0
    past = npg * page

    pos = jnp.concatenate([jnp.arange(tp), past + jnp.tile(jnp.arange(ls), bs)])
    tab_a, half_a = _rope_tables(pos, HEAD_DIM)
    tab_i, half_i = _rope_tables(pos, IDX_DIM)

    w_in_r = _relayout_w_in(w_in, lay)
    w_br_b, w_out_b = w_branch.astype(BF16), w_out.astype(BF16)
    w_up_b, w_down_b = w_up.astype(BF16), w_down.astype(BF16)
    fb_rows = jnp.pad(fox_fb, ((0, 0), (lay.ff_lane, LANES - lay.ff_lane - nh)))[:, None, :]
    g_mix, g_ffn, g_hg = norm_mix_g[:, None, :], norm_ffn_g[:, None, :], hgrn_norm_g[:, None, :]
    as_pages = lambda c: c.reshape(depth, n_pool, page * nh, HEAD_DIM)
    kpool_a, vpool_a, kpool_f, vpool_f = map(as_pages, (cache_dsa_k, cache_dsa_v, cache_fox_k, cache_fox_v))
    new_page = lambda a: jnp.pad(a.reshape(bs, ls * nh, HEAD_DIM), ((0, 0), (0, (page - ls) * nh), (0, 0)))
    ikpool_t = jnp.swapaxes(cache_dsa_kidx, 2, 3)
    lfpool_t = jnp.pad(jnp.swapaxes(cache_fox_logf, 2, 3), ((0, 0), (0, 0), (0, SAMPLE_ROWS - nh), (0, 0)))
    conv_state = state_conv.reshape(depth, bs, (CONV_W - 1) * bw)

    x = jnp.concatenate([x_prompt[0], x_sample.reshape(ts, d)], axis=0)
    new_p, new_s = [], []
    for l in range(depth):
        z = _in_proj(x, g_mix, w_in_r[l], l)
        (aq_r, ak_r, ak_b, av_b, iq_r, ik_r, ik_b, misc2,
         fq_b, fk_b, fv_b) = _post(z, tab_a, tab_i, fb_rows, lay, l, half_a, half_i)
        zs = z[tp:]
        sec = lambda off: zs[:, off:off + bw]

        y_conv_p, u_last = _conv_prompt(z, conv_w, tp, lay, l)
        flat = lambda a: a.reshape(bs, ls * bw)
        y_conv_s, conv_new_s = _conv_sample(flat(sec(lay.ch)), flat(sec(lay.cb)), flat(sec(lay.cc)),
                                            conv_state, conv_w, l, bs, ls, bw)

        y_dsa_p = _dsa_prompt(iq_r, misc2, aq_r, ik_b, ak_b, av_b, tp, lay)
        iq_s = _pad_rows(iq_r[tp:], bs, ls, SAMPLE_ROWS).reshape(bs, SAMPLE_ROWS, IDX_HEADS, LANES)
        iq_s = jnp.swapaxes(iq_s, 1, 2).reshape(bs, IDX_HEADS * SAMPLE_ROWS, LANES)
        iw_s = jnp.swapaxes(_pad_rows(misc2[tp:, :IDX_HEADS], bs, ls, SAMPLE_ROWS), 1, 2)
        iw_s = jnp.broadcast_to(iw_s.reshape(bs, IDX_HEADS * SAMPLE_ROWS, 1), (bs, IDX_HEADS * SAMPLE_ROWS, LANES))
        ik_new = jnp.swapaxes(_pad_rows(ik_r[tp:, :IDX_DIM], bs, ls, page), 1, 2)
        keys, keys_new = _dsa_sample_score(page_table, iq_s, iw_s, ikpool_t, ik_new, l, ls)
        tau = _dsa_sample_thresh(keys, keys_new, ls)
        y_dsa_s = _dsa_sample_attn(page_table, _pad_rows(aq_r[tp:], bs, ls, SAMPLE_ROWS), keys, keys_new, tau,
                                   kpool_a, vpool_a, new_page(ak_b[tp:]), new_page(av_b[tp:]), l, ls, lay)

        crow, ccol = _fox_cumsum(misc2, tp)
        y_fox_p = _fox_prompt(fq_b, fk_b, fv_b, ccol, crow, tp, lay)
        logf_s = misc2[tp:, lay.ff_lane:lay.ff_lane + nh]
        lf_new = jnp.pad(jnp.swapaxes(logf_s.reshape(bs, ls, nh), 1, 2),
                         ((0, 0), (0, SAMPLE_ROWS - nh), (0, page - ls)))
        y_fox_s = _fox_sample(page_table, _pad_rows(fq_b[tp:], bs, ls, SAMPLE_ROWS), kpool_f, vpool_f, lfpool_t,
                              new_page(fk_b[tp:]), new_page(fv_b[tp:]), lf_new, l, ls, lay)

        y_hg_p, s_hg_p = _hgrn_prompt(z, hgrn_lb_logits, g_hg, tp, lay, l)
        pad8 = lambda off: _pad_rows(sec(off), bs, ls, SAMPLE_ROWS)
        y_hg_s, s_hg_s = _hgrn_sample(pad8(lay.hq), pad8(lay.hf), pad8(lay.hi), pad8(lay.hg),
                                      hgrn_lb_logits, g_hg, state_hgrn, l, ls, lay)

        unpad = lambda a: a[:, :ls].reshape(ts, bw)
        branches = [jnp.concatenate([y_conv_p, y_conv_s.reshape(ts, bw)], axis=0),
                    jnp.concatenate([y_dsa_p, unpad(y_dsa_s)], axis=0),
                    jnp.concatenate([y_fox_p, unpad(y_fox_s)], axis=0),
                    jnp.concatenate([y_hg_p, unpad(y_hg_s)], axis=0)]
        x = _mix(branches, z, w_br_b, w_out_b, x, lay, l)
        x = _ffn(x, g_ffn, w_up_b, w_down_b, l)

        heads = lambda a, b_, l_: a.reshape(b_, l_, nh, HEAD_DIM)
        zp = lambda off: z[:tp, off:off + bw]
        new_p.append((heads(ak_r[:tp], 1, tp), heads(zp(lay.av), 1, tp), ik_r[:tp, :IDX_DIM].reshape(1, tp, IDX_DIM),
                      heads(zp(lay.fk), 1, tp), heads(zp(lay.fv), 1, tp),
                      misc2[:tp, lay.ff_lane:lay.ff_lane + nh].reshape(1, tp, nh),
                      u_last[8 - (CONV_W - 1):].reshape(1, CONV_W - 1, bw), s_hg_p[None]))
        new_s.append((heads(ak_r[tp:], bs, ls), heads(sec(lay.av), bs, ls), ik_r[tp:, :IDX_DIM].reshape(bs, ls, IDX_DIM),
                      heads(sec(lay.fk), bs, ls), heads(sec(lay.fv), bs, ls), logf_s.reshape(bs, ls, nh),
                      conv_new_s.reshape(bs, CONV_W - 1, bw), s_hg_s))

    y = _final_norm(x, final_norm_g[None, :])
    y_prompt = y[:tp][None]
    y_sample = y[tp:].reshape(bs, ls, d)
    outs_p = [jnp.stack(t) for t in zip(*new_p)]
    outs_s = [jnp.stack(t) for t in zip(*new_s)]
    return (y_prompt, y_sample, *outs_p, *outs_s)
```

```python
import functools

import jax
import jax.numpy as jnp
from jax import lax
from jax.experimental import pallas as pl
from jax.experimental.pallas import tpu as pltpu

F32 = jnp.float32
BF16 = jnp.bfloat16
I32 = jnp.int32

LANES = 128
HEAD_DIM = 128
IDX_DIM = 64
IDX_HEADS = 16
CONV_W = 3
TOPK_MAX = 256
ROPE_THETA = 500000.0
ROT_FRACTION = 4
EPS = 1e-6
NEG_BIG = -1e30
INT_MIN = -2 ** 31
SAMPLE_ROWS = 8
HG_CHUNK_PROMPT = 16
BISECT_STEPS_PER_CHECK = 4
ATTN_PAGES_PER_STEP = (16, 8, 4, 2, 1)
VMEM_LIMIT = 56 * 1024 * 1024
HIGHEST = lax.Precision.HIGHEST


def _dot(a, b, precision=None):
    return jnp.dot(a, b, preferred_element_type=F32, precision=precision)


def _dot_nt(a, b):
    return lax.dot_general(a, b, (((1,), (1,)), ((), ())), preferred_element_type=F32)


def _dot_tn(a, b):
    return lax.dot_general(a, b, (((0,), (0,)), ((), ())), preferred_element_type=F32)


def _pick(n, cands):
    for c in cands:
        if n % c == 0:
            return c
    raise ValueError(f"no tile for {n} in {cands}")


def _params(sem, vmem=VMEM_LIMIT):
    return pltpu.CompilerParams(dimension_semantics=sem, vmem_limit_bytes=vmem)


def _log_sigmoid(x):
    return jnp.minimum(x, 0.0) - jnp.log1p(jnp.exp(-jnp.abs(x)))


def _rms(x, g):
    return x * lax.rsqrt(jnp.mean(x * x, axis=-1, keepdims=True) + EPS) * g


class _Layout:
    def __init__(self, d_model):
        bw = d_model // 4
        assert bw % HEAD_DIM == 0
        self.d = d_model
        self.bw = bw
        self.nh = bw // HEAD_DIM
        self.iqw = IDX_HEADS * LANES
        assert self.iqw % bw == 0 and IDX_HEADS + self.nh <= LANES
        o = 0
        for name in ("ch", "cb", "cc", "aq", "ak", "av", "fq", "fk"):
            setattr(self, name, o)
            o += bw
        self.iq = o
        o += self.iqw
        for name in ("fv", "hq", "hf", "hi", "hg"):
            setattr(self, name, o)
            o += bw
        self.ikp = o
        o += LANES
        self.misc = o
        o += LANES
        o = -(-o // bw) * bw
        self.gate = o
        o += 4 * d_model
        self.n = o
        self.ff_lane = IDX_HEADS


def _relayout_w_in(w_in, lay):
    bw, nh = lay.bw, lay.nh
    dep, d, _ = w_in.shape
    wt = jnp.transpose(w_in, (2, 0, 1))
    o = 0

    def take(width):
        nonlocal o
        s = wt[o:o + width]
        o += width
        return s

    first6 = take(6 * bw)
    iq = take(IDX_HEADS * IDX_DIM)
    ik = take(IDX_DIM)
    iw = take(IDX_HEADS)
    fq, fk, fv = take(bw), take(bw), take(bw)
    ff = take(nh)
    h4 = take(4 * bw)
    gate = take(4 * lay.d)
    assert o == wt.shape[0]
    iq_pad = jnp.pad(iq.reshape(IDX_HEADS, IDX_DIM, dep, d),
                     ((0, 0), (0, LANES - IDX_DIM), (0, 0), (0, 0))).reshape(lay.iqw, dep, d)
    ik_pad = jnp.pad(ik, ((0, LANES - IDX_DIM), (0, 0), (0, 0)))
    misc = jnp.pad(jnp.concatenate([iw, ff], axis=0), ((0, LANES - IDX_HEADS - nh), (0, 0), (0, 0)))
    pad = jnp.zeros((lay.gate - lay.misc - LANES, dep, d), w_in.dtype)
    out = jnp.concatenate([first6, fq, fk, iq_pad, fv, h4, ik_pad, misc, pad, gate], axis=0)
    assert out.shape[0] == lay.n
    return [out[:, l, :].astype(BF16) for l in range(dep)]


def _rope_tables(pos, head_dim):
    d_rot = head_dim // ROT_FRACTION
    half = d_rot // 2
    inv = ROPE_THETA ** (-jnp.arange(half, dtype=F32) / half)
    ang = pos.astype(F32)[:, None] * inv[None, :]
    cos, sin = jnp.cos(ang), jnp.sin(ang)
    t = pos.shape[0]
    rest = LANES - d_rot
    c = jnp.concatenate([cos, cos, jnp.ones((t, rest), F32)], axis=1)
    s1 = jnp.concatenate([-sin, jnp.zeros((t, half + rest), F32)], axis=1)
    s2 = jnp.concatenate([jnp.zeros((t, half), F32), sin, jnp.zeros((t, rest), F32)], axis=1)
    return jnp.concatenate([c, s1, s2], axis=1), half


def _in_proj_kernel(x_ref, g_ref, w_ref, z_ref, xn_ref):
    @pl.when(pl.program_id(1) == 0)
    def _():
        xn_ref[...] = _rms(x_ref[...], g_ref[...]).astype(BF16)

    z_ref[...] = _dot_nt(xn_ref[...], w_ref[...])


def _in_proj(x, g, w_r, layer):
    t, d = x.shape
    n = w_r.shape[0]
    tm = _pick(t, (640, 512, 256, 128))
    tn = _pick(n, (1024, 512))
    return pl.pallas_call(
        _in_proj_kernel,
        grid=(t // tm, n // tn),
        in_specs=[pl.BlockSpec((tm, d), lambda i, j: (i, 0)),
                  pl.BlockSpec((None, 1, d), lambda i, j: (layer, 0, 0)),
                  pl.BlockSpec((tn, d), lambda i, j: (j, 0))],
        out_specs=pl.BlockSpec((tm, tn), lambda i, j: (i, j)),
        out_shape=jax.ShapeDtypeStruct((t, n), F32),
        scratch_shapes=[pltpu.VMEM((tm, d), BF16)],
        compiler_params=_params(("parallel", "arbitrary")),
        name="in_proj",
    )(x, g, w_r)


def _rope(x, tab, half, reps):
    w = x.shape[1]
    c = jnp.concatenate([tab[:, 0:LANES]] * reps, axis=1)
    s1 = jnp.concatenate([tab[:, LANES:2 * LANES]] * reps, axis=1)
    s2 = jnp.concatenate([tab[:, 2 * LANES:3 * LANES]] * reps, axis=1)
    return x * c + pltpu.roll(x, w - half, 1) * s1 + pltpu.roll(x, half, 1) * s2


def _post_kernel(aq_ref, ak_ref, av_ref, iq_ref, ikp_ref, misc_ref, fq_ref, fk_ref, fv_ref,
                 ta_ref, ti_ref, fb_ref,
                 aqr_ref, akr_ref, akb_ref, avb_ref, iqr_ref, ikr_ref, ikb_ref, m2_ref,
                 fqb_ref, fkb_ref, fvb_ref, *, nh, half_a, half_i, idx_scale):
    ta, ti = ta_ref[...], ti_ref[...]
    fqb_ref[...] = fq_ref[...].astype(BF16)
    fkb_ref[...] = fk_ref[...].astype(BF16)
    fvb_ref[...] = fv_ref[...].astype(BF16)
    aqr_ref[...] = _rope(aq_ref[...], ta, half_a, nh).astype(BF16)
    akr = _rope(ak_ref[...], ta, half_a, nh)
    akr_ref[...] = akr
    akb_ref[...] = akr.astype(BF16)
    avb_ref[...] = av_ref[...].astype(BF16)
    iqr_ref[...] = _rope(iq_ref[...], ti, half_i, IDX_HEADS).astype(BF16)
    ikr = _rope(ikp_ref[...], ti, half_i, 1)
    ikr_ref[...] = ikr
    ikb_ref[...] = ikr.astype(BF16)
    m = misc_ref[...]
    lane = lax.broadcasted_iota(I32, m.shape, 1)
    logf = _log_sigmoid(m + fb_ref[...])
    m2_ref[...] = jnp.where(lane < IDX_HEADS, m * idx_scale,
                            jnp.where(lane < IDX_HEADS + nh, logf, 0.0))


def _post(z, tab_a, tab_i, fb_row, lay, layer, half_a, half_i):
    t = z.shape[0]
    bw = lay.bw
    tm = _pick(t, (320, 256, 128, 64, 32, 16))
    col = lambda off, w: pl.BlockSpec((tm, w), lambda i: (i, off // w))
    row = lambda w: pl.BlockSpec((tm, w), lambda i: (i, 0))
    kern = functools.partial(_post_kernel, nh=lay.nh, half_a=half_a, half_i=half_i,
                             idx_scale=float((IDX_HEADS * IDX_DIM) ** -0.5))
    return pl.pallas_call(
        kern,
        grid=(t // tm,),
        in_specs=[col(lay.aq, bw), col(lay.ak, bw), col(lay.av, bw), col(lay.iq, lay.iqw),
                  col(lay.ikp, LANES), col(lay.misc, LANES), col(lay.fq, bw), col(lay.fk, bw), col(lay.fv, bw),
                  row(3 * LANES), row(3 * LANES),
                  pl.BlockSpec((None, 1, LANES), lambda i: (layer, 0, 0))],
        out_specs=[row(bw), row(bw), row(bw), row(bw), row(lay.iqw), row(LANES), row(LANES), row(LANES),
                   row(bw), row(bw), row(bw)],
        out_shape=[jax.ShapeDtypeStruct((t, bw), BF16), jax.ShapeDtypeStruct((t, bw), F32),
                   jax.ShapeDtypeStruct((t, bw), BF16), jax.ShapeDtypeStruct((t, bw), BF16),
                   jax.ShapeDtypeStruct((t, lay.iqw), BF16), jax.ShapeDtypeStruct((t, LANES), F32),
                   jax.ShapeDtypeStruct((t, LANES), BF16), jax.ShapeDtypeStruct((t, LANES), F32),
                   jax.ShapeDtypeStruct((t, bw), BF16), jax.ShapeDtypeStruct((t, bw), BF16),
                   jax.ShapeDtypeStruct((t, bw), BF16)],
        compiler_params=_params(("parallel",)),
        name="post_proj",
    )(z, z, z, z, z, z, z, z, z, tab_a, tab_i, fb_row)


def _mix_kernel(b0_ref, b1_ref, b2_ref, b3_ref, g0_ref, g1_ref, g2_ref, g3_ref,
                wbr_ref, wout_ref, x_ref, o_ref, acc_ref):
    dt = pl.program_id(1)

    @pl.when(dt == 0)
    def _():
        acc_ref[...] = jnp.zeros_like(acc_ref)

    m = None
    for n, (b_ref, g_ref) in enumerate(((b0_ref, g0_ref), (b1_ref, g1_ref), (b2_ref, g2_ref), (b3_ref, g3_ref))):
        term = jax.nn.sigmoid(g_ref[...]) * _dot(b_ref[...], wbr_ref[n])
        m = term if m is None else m + term
    acc_ref[...] += _dot(m.astype(BF16), wout_ref[...])

    @pl.when(dt == pl.num_programs(1) - 1)
    def _():
        o_ref[...] = x_ref[...] + acc_ref[...]


def _mix(branches, z, w_br, w_out, x, lay, layer):
    t, d = x.shape
    bw = lay.bw
    tm = _pick(t, (640, 512, 256, 128))
    td = 512
    assert d % td == 0 and lay.gate % td == 0
    gspec = lambda n: pl.BlockSpec((tm, td), lambda i, j: (i, (lay.gate + n * d) // td + j))
    bspec = pl.BlockSpec((tm, bw), lambda i, j: (i, 0))
    return pl.pallas_call(
        _mix_kernel,
        grid=(t // tm, d // td),
        in_specs=[bspec, bspec, bspec, bspec, gspec(0), gspec(1), gspec(2), gspec(3),
                  pl.BlockSpec((None, 4, bw, td), lambda i, j: (layer, 0, 0, j)),
                  pl.BlockSpec((None, td, d), lambda i, j: (layer, j, 0)),
                  pl.BlockSpec((tm, d), lambda i, j: (i, 0))],
        out_specs=pl.BlockSpec((tm, d), lambda i, j: (i, 0)),
        out_shape=jax.ShapeDtypeStruct((t, d), F32),
        scratch_shapes=[pltpu.VMEM((tm, d), F32)],
        compiler_params=_params(("parallel", "arbitrary")),
        name="branch_mix",
    )(*branches, z, z, z, z, w_br, w_out, x)


def _ffn_kernel(x_ref, g_ref, wu_ref, wd_ref, o_ref, xn_ref, acc_ref):
    f = pl.program_id(1)

    @pl.when(f == 0)
    def _():
        xn_ref[...] = _rms(x_ref[...], g_ref[...]).astype(BF16)
        acc_ref[...] = jnp.zeros_like(acc_ref)

    h = jnp.maximum(_dot(xn_ref[...], wu_ref[...]), 0.0)
    acc_ref[...] += _dot((h * h).astype(BF16), wd_ref[...])

    @pl.when(f == pl.num_programs(1) - 1)
    def _():
        o_ref[...] = x_ref[...] + acc_ref[...]


def _ffn(x, g, w_up, w_down, layer):
    t, d = x.shape
    dff = w_up.shape[-1]
    tm = _pick(t, (640, 512, 256, 128))
    tf = _pick(dff, (1024, 512, 256, 128))
    return pl.pallas_call(
        _ffn_kernel,
        grid=(t // tm, dff // tf),
        in_specs=[pl.BlockSpec((tm, d), lambda i, j: (i, 0)),
                  pl.BlockSpec((None, 1, d), lambda i, j: (layer, 0, 0)),
                  pl.BlockSpec((None, d, tf), lambda i, j: (layer, 0, j)),
                  pl.BlockSpec((None, tf, d), lambda i, j: (layer, j, 0))],
        out_specs=pl.BlockSpec((tm, d), lambda i, j: (i, 0)),
        out_shape=jax.ShapeDtypeStruct((t, d), F32),
        scratch_shapes=[pltpu.VMEM((tm, d), BF16), pltpu.VMEM((tm, d), F32)],
        compiler_params=_params(("parallel", "arbitrary")),
        name="ffn",
    )(x, g, w_up, w_down)


def _final_norm_kernel(x_ref, g_ref, o_ref):
    o_ref[...] = _rms(x_ref[...], g_ref[...])


def _final_norm(x, g):
    t, d = x.shape
    tm = _pick(t, (640, 512, 256, 128))
    return pl.pallas_call(
        _final_norm_kernel,
        grid=(t // tm,),
        in_specs=[pl.BlockSpec((tm, d), lambda i: (i, 0)), pl.BlockSpec((1, d), lambda i: (0, 0))],
        out_specs=pl.BlockSpec((tm, d), lambda i: (i, 0)),
        out_shape=jax.ShapeDtypeStruct((t, d), F32),
        compiler_params=_params(("parallel",)),
        name="final_norm",
    )(x, g)


def _conv_prompt_kernel(ch_ref, cb_ref, cc_ref, w_ref, y_ref, last_ref, carry_ref):
    @pl.when(pl.program_id(0) == 0)
    def _():
        carry_ref[...] = jnp.zeros_like(carry_ref)

    u = cc_ref[...] * ch_ref[...]
    tm = u.shape[0]
    rid = lax.broadcasted_iota(I32, u.shape, 0)
    prev = carry_ref[...]
    acc = w_ref[CONV_W - 1:CONV_W, :] * u
    for back in range(1, CONV_W):
        head = jnp.concatenate([pltpu.roll(prev, back, 0), jnp.zeros((tm - 8, u.shape[1]), F32)], axis=0)
        shifted = jnp.where(rid < back, head, pltpu.roll(u, back, 0))
        acc = acc + w_ref[CONV_W - 1 - back:CONV_W - back, :] * shifted
    y_ref[...] = (cb_ref[...] * acc).astype(y_ref.dtype)
    carry_ref[...] = u[tm - 8:tm, :]
    last_ref[...] = u[tm - 8:tm, :]


def _conv_prompt(z, conv_w, tp, lay, layer):
    bw = lay.bw
    tm = _pick(tp, (512, 256, 128))
    col = lambda off: pl.BlockSpec((tm, bw), lambda i: (i, off // bw))
    return pl.pallas_call(
        _conv_prompt_kernel,
        grid=(tp // tm,),
        in_specs=[col(lay.ch), col(lay.cb), col(lay.cc),
                  pl.BlockSpec((None, CONV_W, bw), lambda i: (layer, 0, 0))],
        out_specs=[pl.BlockSpec((tm, bw), lambda i: (i, 0)), pl.BlockSpec((8, bw), lambda i: (0, 0))],
        out_shape=[jax.ShapeDtypeStruct((tp, bw), BF16), jax.ShapeDtypeStruct((8, bw), F32)],
        scratch_shapes=[pltpu.VMEM((8, bw), F32)],
        compiler_params=_params(("arbitrary",)),
        name="conv_prompt",
    )(z, z, z, conv_w)


def _conv_sample_kernel(ch_ref, cb_ref, cc_ref, st_ref, w_ref, y_ref, ns_ref, *, ls, bw):
    u_ext = jnp.concatenate([st_ref[...], cc_ref[...] * ch_ref[...]], axis=1)
    cb = cb_ref[...]
    ys = []
    for t in range(ls):
        acc = None
        for j in range(CONV_W):
            term = w_ref[j:j + 1, :] * u_ext[:, (t + j) * bw:(t + j + 1) * bw]
            acc = term if acc is None else acc + term
        ys.append(cb[:, t * bw:(t + 1) * bw] * acc)
    y_ref[...] = jnp.concatenate(ys, axis=1).astype(y_ref.dtype)
    ns_ref[...] = u_ext[:, ls * bw:(ls + CONV_W - 1) * bw]


def _conv_sample(ch, cb, cc, state, conv_w, layer, bs, ls, bw):
    kern = functools.partial(_conv_sample_kernel, ls=ls, bw=bw)
    full = lambda shape: pl.BlockSpec(shape, lambda i: (0,) * len(shape))
    return pl.pallas_call(
        kern,
        grid=(1,),
        in_specs=[full((bs, ls * bw)), full((bs, ls * bw)), full((bs, ls * bw)),
                  pl.BlockSpec((None, bs, (CONV_W - 1) * bw), lambda i: (layer, 0, 0)),
                  pl.BlockSpec((None, CONV_W, bw), lambda i: (layer, 0, 0))],
        out_specs=[full((bs, ls * bw)), full((bs, (CONV_W - 1) * bw))],
        out_shape=[jax.ShapeDtypeStruct((bs, ls * bw), BF16),
                   jax.ShapeDtypeStruct((bs, (CONV_W - 1) * bw), F32)],
        compiler_params=_params(("arbitrary",)),
        name="conv_sample",
    )(ch, cb, cc, state, conv_w)


def _f32_key(x):
    bits = lax.bitcast_convert_type(x, I32)
    return jnp.where(bits < 0, bits ^ jnp.int32(0x7FFFFFFF), bits)


def _topk_select(key, col, tau, jmax):
    return (key > tau) | ((key == tau) & (col <= jmax))


def _topk_threshold(get_chunk, nch, k, n_valid, rows, width, jmax_ref):
    def count(pred):
        def body(c, part):
            m = jnp.where(pred(c, get_chunk(c)), 1.0, 0.0)
            blocks = [m[:, b * LANES:(b + 1) * LANES] for b in range(width // LANES)]
            while len(blocks) > 1:
                blocks = [a + b for a, b in zip(blocks[0::2], blocks[1::2])] + blocks[len(blocks) & ~1:]
            return part + blocks[0]

        part = lax.fori_loop(0, nch, body, jnp.zeros((rows, LANES), F32))
        return jnp.sum(part, axis=1, keepdims=True)

    kf = float(k)
    take_all = n_valid <= kf
    cnt0 = count(lambda c, kb: kb >= 0)
    base0 = jnp.where(cnt0 >= kf, jnp.int32(0), jnp.int32(INT_MIN))
    done0 = jnp.where(take_all | (cnt0 == kf), 1, 0).astype(I32)

    def left(done):
        return jnp.sum(jnp.where(done > 0, 0.0, 1.0))

    def cond(st):
        it, _, _, nleft = st
        return (it < 31) & (nleft > 0.0)

    def body(st):
        it, base, done, _ = st
        for u in range(BISECT_STEPS_PER_CHECK):
            bit = 30 - it - u
            cand = base | jnp.where(bit >= 0, jnp.left_shift(jnp.int32(1), jnp.maximum(bit, 0)), 0)
            cnt = count(lambda c, kb, cand=cand: kb >= cand)
            base = jnp.where(cnt >= kf, cand, base)
            done = jnp.where(cnt == kf, 1, done)
        return it + BISECT_STEPS_PER_CHECK, base, done, left(done)

    _, base, done, nleft = lax.while_loop(cond, body, (jnp.int32(0), base0, done0, left(done0)))
    tau = jnp.where(take_all, jnp.int32(INT_MIN), base)
    jmax_ref[...] = jnp.full(jmax_ref.shape, 2 ** 31 - 1, I32)

    @pl.when(nleft > 0.0)
    def _():
        lane = lax.broadcasted_iota(I32, (rows, width), 1)
        need = kf - count(lambda c, kb: kb > tau)

        def idx_body(it, lo):
            cand = lo | jnp.left_shift(jnp.int32(1), 30 - it)
            cnt = count(lambda c, kb: (kb == tau) & (c * width + lane < cand))
            return jnp.where(cnt < need, cand, lo)

        lo = lax.fori_loop(0, 31, idx_body, jnp.zeros((rows, 1), I32))
        jmax = jnp.where(done > 0, jnp.int32(2 ** 31 - 1), lo)
        jmax_ref[...] = jnp.broadcast_to(jmax, jmax_ref.shape)

    return tau, jmax_ref[:, 0:1]


def _dsa_prompt_kernel(iq_ref, mw_ref, aq_ref, ik_ref, ak_ref, av_ref, o_ref, sc_ref, jm_ref,
                       *, tq, tk, n_sel, nh, scale):
    i = pl.program_id(0)
    nch = ((i + 1) * tq + tk - 1) // tk
    row = i * tq + lax.broadcasted_iota(I32, (tq, tk), 0)
    col0 = lax.broadcasted_iota(I32, (tq, tk), 1)
    w = mw_ref[...]

    def score_body(c, carry):
        off = pl.multiple_of(c * tk, tk)
        ikc = ik_ref[pl.ds(off, tk), :]
        acc = jnp.zeros((tq, tk), F32)
        for h in range(IDX_HEADS):
            d = _dot_nt(iq_ref[:, h * LANES:(h + 1) * LANES], ikc)
            acc = acc + w[:, h:h + 1] * jnp.maximum(d, 0.0)
        s = jnp.where(off + col0 <= row, acc + 0.0, NEG_BIG)
        sc_ref[c] = _f32_key(s)
        return carry

    lax.fori_loop(0, nch, score_body, 0)
    n_valid = (row[:, 0:1] + 1).astype(F32)
    tau, jmax = _topk_threshold(lambda c: sc_ref[c], nch, n_sel, n_valid, tq, tk, jm_ref)

    def att_body(c, carry):
        off = pl.multiple_of(c * tk, tk)
        col = off + col0
        bias = jnp.where(_topk_select(sc_ref[c], col, tau, jmax) & (col <= row), 0.0, NEG_BIG)
        new = []
        for h in range(nh):
            m, l, acc = carry[h]
            hs = slice(h * HEAD_DIM, (h + 1) * HEAD_DIM)
            s = _dot_nt(aq_ref[:, hs], ak_ref[pl.ds(off, tk), hs]) * scale + bias
            m_new = jnp.maximum(m, jnp.max(s, axis=1, keepdims=True))
            p = jnp.exp(s - m_new)
            alpha = jnp.exp(m - m_new)
            l = l * alpha + jnp.sum(p, axis=1, keepdims=True)
            acc = acc * alpha + _dot(p.astype(BF16), av_ref[pl.ds(off, tk), hs])
            new.append((m_new, l, acc))
        return tuple(new)

    init = tuple((jnp.full((tq, 1), NEG_BIG, F32), jnp.zeros((tq, 1), F32), jnp.zeros((tq, HEAD_DIM), F32))
                 for _ in range(nh))
    fin = lax.fori_loop(0, nch, att_body, init)
    o_ref[...] = jnp.concatenate([acc / l for _, l, acc in fin], axis=1).astype(o_ref.dtype)


def _dsa_prompt(iq_r, misc2, aq_r, ik_b, ak_b, av_b, tp, lay):
    bw = lay.bw
    tq = 256
    tk = _pick(tp, (512, 256, 128))
    n_sel = min(TOPK_MAX, tp // 4)
    kern = functools.partial(_dsa_prompt_kernel, tq=tq, tk=tk, n_sel=n_sel, nh=lay.nh,
                             scale=float(HEAD_DIM ** -0.5))
    res = lambda w: pl.BlockSpec((tp, w), lambda i: (0, 0))
    return pl.pallas_call(
        kern,
        grid=(tp // tq,),
        in_specs=[pl.BlockSpec((tq, lay.iqw), lambda i: (i, 0)),
                  pl.BlockSpec((tq, LANES), lambda i: (i, 0)),
                  pl.BlockSpec((tq, bw), lambda i: (i, 0)),
                  res(LANES), res(bw), res(bw)],
        out_specs=pl.BlockSpec((tq, bw), lambda i: (i, 0)),
        out_shape=jax.ShapeDtypeStruct((tp, bw), BF16),
        scratch_shapes=[pltpu.VMEM((tp // tk, tq, tk), I32), pltpu.VMEM((tq, LANES), I32)],
        compiler_params=_params(("arbitrary",)),
        name="dsa_prompt",
    )(iq_r, misc2, aq_r, ik_b, ak_b, av_b)


def _new_key_valid(shape, ls):
    t = lax.broadcasted_iota(I32, shape, 0) & (SAMPLE_ROWS - 1)
    c = lax.broadcasted_iota(I32, shape, 1)
    return (c <= t) & (c < ls)


def _page_spec(shape, layer, g, gi):
    return pl.BlockSpec((None, None) + shape, lambda b, j, pt: (layer, pt[b, j * g + gi], 0, 0))


def _dsa_sample_score_kernel(pt_ref, iq_ref, iw_ref, *rest, npg, g, page, ls):
    pool_refs = rest[:g]
    new_ref, key_ref, keyn_ref = rest[g:]
    j = pl.program_id(1)

    def index_keys(keys_t, valid):
        d = _dot(iq_ref[:, 0:IDX_DIM], keys_t.astype(BF16))
        r = jnp.maximum(d, 0.0) * iw_ref[:, 0:1]
        s = r[0:SAMPLE_ROWS, :]
        for h in range(1, IDX_HEADS):
            s = s + r[h * SAMPLE_ROWS:(h + 1) * SAMPLE_ROWS, :]
        s = s + 0.0
        return _f32_key(s if valid is None else jnp.where(valid, s, NEG_BIG))

    key_ref[...] = index_keys(jnp.concatenate([r[...] for r in pool_refs], axis=1), None)

    @pl.when(j == npg // g - 1)
    def _():
        keyn_ref[...] = index_keys(new_ref[...], _new_key_valid((SAMPLE_ROWS, page), ls))


def _dsa_sample_score(page_table, iq_s, iw_s, pool, ik_new, layer, ls):
    bs, npg = page_table.shape
    page = pool.shape[3]
    g = _pick(npg, ATTN_PAGES_PER_STEP)
    kern = functools.partial(_dsa_sample_score_kernel, npg=npg, g=g, page=page, ls=ls)
    per_seq = lambda shape: pl.BlockSpec((None,) + shape, lambda b, j, pt: (b, 0, 0))
    grid_spec = pltpu.PrefetchScalarGridSpec(
        num_scalar_prefetch=1,
        grid=(bs, npg // g),
        in_specs=[per_seq((IDX_HEADS * SAMPLE_ROWS, LANES)), per_seq((IDX_HEADS * SAMPLE_ROWS, LANES))]
        + [_page_spec((IDX_DIM, page), layer, g, gi) for gi in range(g)]
        + [per_seq((IDX_DIM, page))],
        out_specs=[pl.BlockSpec((None, SAMPLE_ROWS, g * page), lambda b, j, pt: (b, 0, j)),
                   per_seq((SAMPLE_ROWS, page))],
    )
    return pl.pallas_call(
        kern,
        grid_spec=grid_spec,
        out_shape=[jax.ShapeDtypeStruct((bs, SAMPLE_ROWS, npg * page), I32),
                   jax.ShapeDtypeStruct((bs, SAMPLE_ROWS, page), I32)],
        compiler_params=_params(("parallel", "arbitrary")),
        name="dsa_sample_score",
    )(page_table, iq_s, iw_s, *([pool] * g), ik_new)


def _dsa_sample_thresh_kernel(key_ref, keyn_ref, tau_ref, sc_ref, jm_ref, *, nseq, past, ls, n_sel):
    rows = nseq * SAMPLE_ROWS
    width = sc_ref.shape[2]
    sc_ref[0] = jnp.concatenate([key_ref[...].reshape(rows, key_ref.shape[2]),
                                 keyn_ref[...].reshape(rows, keyn_ref.shape[2])], axis=1)
    t = lax.broadcasted_iota(I32, (rows, 1), 0) & (SAMPLE_ROWS - 1)
    n_valid = (past + jnp.minimum(t + 1, ls)).astype(F32)
    tau, jmax = _topk_threshold(lambda c: sc_ref[c], 1, n_sel, n_valid, rows, width, jm_ref)
    lane = lax.broadcasted_iota(I32, (rows, LANES), 1)
    tau_ref[...] = jnp.where(lane == 0, tau, jmax).reshape(tau_ref.shape)


def _dsa_sample_thresh(keys, keys_new, ls):
    bs, _, past = keys.shape
    page = keys_new.shape[2]
    nseq = _pick(bs, (4, 2, 1))
    n_sel = min(TOPK_MAX, (past + ls) // 4)
    kern = functools.partial(_dsa_sample_thresh_kernel, nseq=nseq, past=past, ls=ls, n_sel=n_sel)
    blk = lambda w: pl.BlockSpec((nseq, SAMPLE_ROWS, w), lambda b: (b, 0, 0))
    return pl.pallas_call(
        kern,
        grid=(bs // nseq,),
        in_specs=[blk(past), blk(page)],
        out_specs=blk(LANES),
        out_shape=jax.ShapeDtypeStruct((bs, SAMPLE_ROWS, LANES), I32),
        scratch_shapes=[pltpu.VMEM((1, nseq * SAMPLE_ROWS, past + page), I32),
                        pltpu.VMEM((nseq * SAMPLE_ROWS, LANES), I32)],
        compiler_params=_params(("parallel",)),
        name="dsa_sample_thresh",
    )(keys, keys_new)


def _stack_heads(q, nh):
    return jnp.concatenate([q[:, h * HEAD_DIM:(h + 1) * HEAD_DIM] for h in range(nh)], axis=0)


def _unstack_heads(o, nh):
    return jnp.concatenate([o[h * SAMPLE_ROWS:(h + 1) * SAMPLE_ROWS, :] for h in range(nh)], axis=1)


def _page_patterns(page, nh):
    shift = nh.bit_length() - 1
    assert 1 << shift == nh
    shape = (nh * SAMPLE_ROWS, page * nh)
    r = lax.broadcasted_iota(I32, shape, 0)
    c = lax.broadcasted_iota(I32, shape, 1)
    own_head = (c & (nh - 1)) == (r >> 3)
    new_valid = lambda ls: ((c >> shift) <= (r & (SAMPLE_ROWS - 1))) & ((c >> shift) < ls)
    er = lax.broadcasted_iota(I32, (page, page * nh), 0)
    ec = lax.broadcasted_iota(I32, (page, page * nh), 1)
    expand = jnp.where((ec >> shift) == er, 1.0, 0.0)
    return own_head, new_valid, expand


def _online_softmax_step(logits, masks, values, m_ref, l_ref, acc_ref):
    m = m_ref[...]
    m_new = m
    for s in logits:
        m_new = jnp.maximum(m_new, jnp.max(s, axis=1, keepdims=True))
    alpha = jnp.exp(m - m_new)
    l = l_ref[...] * alpha
    acc = acc_ref[...] * alpha
    for s, msk, v in zip(logits, masks, values):
        p = jnp.where(msk, jnp.exp(s - m_new), 0.0)
        l = l + jnp.sum(p, axis=1, keepdims=True)
        acc = acc + _dot(p.astype(BF16), v)
    l_ref[...] = l
    acc_ref[...] = acc
    m_ref[...] = m_new


def _sample_attn_init(q_ref, q2_ref, m_ref, l_ref, acc_ref, nh):
    q2_ref[...] = _stack_heads(q_ref[...], nh).astype(BF16)
    m_ref[...] = jnp.full_like(m_ref, NEG_BIG)
    l_ref[...] = jnp.zeros_like(l_ref)
    acc_ref[...] = jnp.zeros_like(acc_ref)


def _sample_attn_scratch(nh):
    rows = nh * SAMPLE_ROWS
    return [pltpu.VMEM((rows, HEAD_DIM), BF16), pltpu.VMEM((rows, 1), F32),
            pltpu.VMEM((rows, 1), F32), pltpu.VMEM((rows, HEAD_DIM), F32)]


def _dsa_sample_attn_kernel(pt_ref, q_ref, key_ref, keyn_ref, tau_ref, *rest, npg, g, page, ls, nh, scale):
    kp_refs, vp_refs = rest[:g], rest[g:2 * g]
    kn_ref, vn_ref, o_ref, q2_ref, m_ref, l_ref, acc_ref = rest[2 * g:]
    j = pl.program_id(1)

    @pl.when(j == 0)
    def _():
        _sample_attn_init(q_ref, q2_ref, m_ref, l_ref, acc_ref, nh)

    own_head, new_valid, expand = _page_patterns(page, nh)
    expand = expand.astype(BF16)
    tau, jmax = tau_ref[:, 0:1], tau_ref[:, 1:2]
    lane = lax.broadcasted_iota(I32, (SAMPLE_ROWS, page), 1)

    def attend(kps, vps, key8s, col0s, valid):
        sel8 = jnp.concatenate([jnp.where(_topk_select(k8, c0 + lane, tau, jmax), 1.0, 0.0)
                                for k8, c0 in zip(key8s, col0s)], axis=0).astype(BF16)
        sel_x = _dot(sel8, expand) > 0.5
        q2 = q2_ref[...]
        logits, masks = [], []
        for n, kp in enumerate(kps):
            sel = jnp.concatenate([sel_x[n * SAMPLE_ROWS:(n + 1) * SAMPLE_ROWS, :]] * nh, axis=0)
            msk = sel & own_head if valid is None else sel & own_head & valid
            logits.append(jnp.where(msk, _dot_nt(q2, kp.astype(BF16)) * scale, NEG_BIG))
            masks.append(msk)
        _online_softmax_step(logits, masks, [vp.astype(BF16) for vp in vps], m_ref, l_ref, acc_ref)

    attend([r[...] for r in kp_refs], [r[...] for r in vp_refs],
           [key_ref[:, gi * page:(gi + 1) * page] for gi in range(g)],
           [(j * g + gi) * page for gi in range(g)], None)

    @pl.when(j == npg // g - 1)
    def _():
        attend([kn_ref[...]], [vn_ref[...]], [keyn_ref[...]], [npg * page], new_valid(ls))
        o_ref[...] = _unstack_heads(acc_ref[...] / l_ref[...], nh).astype(o_ref.dtype)


def _dsa_sample_attn(page_table, q_s, keys, keys_new, tau, kpool, vpool, k_new, v_new, layer, ls, lay):
    bs, npg = page_table.shape
    rows = kpool.shape[2]
    bw, nh = lay.bw, lay.nh
    page = rows // nh
    g = _pick(npg, ATTN_PAGES_PER_STEP)
    kern = functools.partial(_dsa_sample_attn_kernel, npg=npg, g=g, page=page, ls=ls, nh=nh,
                             scale=float(HEAD_DIM ** -0.5))
    per_seq = lambda shape: pl.BlockSpec((None,) + shape, lambda b, j, pt: (b, 0, 0))
    pages = [_page_spec((rows, HEAD_DIM), layer, g, gi) for gi in range(g)]
    grid_spec = pltpu.PrefetchScalarGridSpec(
        num_scalar_prefetch=1,
        grid=(bs, npg // g),
        in_specs=[per_seq((SAMPLE_ROWS, bw)),
                  pl.BlockSpec((None, SAMPLE_ROWS, g * page), lambda b, j, pt: (b, 0, j)),
                  per_seq((SAMPLE_ROWS, page)), per_seq((SAMPLE_ROWS, LANES))]
        + pages + pages + [per_seq((rows, HEAD_DIM)), per_seq((rows, HEAD_DIM))],
        out_specs=per_seq((SAMPLE_ROWS, bw)),
        scratch_shapes=_sample_attn_scratch(nh),
    )
    return pl.pallas_call(
        kern,
        grid_spec=grid_spec,
        out_shape=jax.ShapeDtypeStruct((bs, SAMPLE_ROWS, bw), BF16),
        compiler_params=_params(("parallel", "arbitrary")),
        name="dsa_sample_attn",
    )(page_table, q_s, keys, keys_new, tau, *([kpool] * g), *([vpool] * g), k_new, v_new)


def _upper_ones(n):
    r = lax.broadcasted_iota(I32, (n, n), 0)
    c = lax.broadcasted_iota(I32, (n, n), 1)
    return jnp.where(r <= c, 1.0, 0.0).astype(F32)


def _fox_cumsum_kernel(m2_ref, crow_ref, ccol_ref, carry_ref):
    @pl.when(pl.program_id(0) == 0)
    def _():
        carry_ref[...] = jnp.zeros_like(carry_ref)

    lf_t = m2_ref[...].T
    tc = lf_t.shape[1]
    cs = _dot(lf_t, _upper_ones(tc), HIGHEST) + carry_ref[:, 0:1]
    crow_ref[...] = cs
    ccol_ref[...] = cs.T
    carry_ref[...] = jnp.broadcast_to(cs[:, tc - 1:tc], carry_ref.shape)


def _fox_cumsum(misc2, tp):
    tc = _pick(tp, (512, 256, 128))
    return pl.pallas_call(
        _fox_cumsum_kernel,
        grid=(tp // tc,),
        in_specs=[pl.BlockSpec((tc, LANES), lambda i: (i, 0))],
        out_specs=[pl.BlockSpec((LANES, tc), lambda i: (0, i)), pl.BlockSpec((tc, LANES), lambda i: (i, 0))],
        out_shape=[jax.ShapeDtypeStruct((LANES, tp), F32), jax.ShapeDtypeStruct((tp, LANES), F32)],
        scratch_shapes=[pltpu.VMEM((LANES, LANES), F32)],
        compiler_params=_params(("arbitrary",)),
        name="fox_cumsum",
    )(misc2)


def _fox_prompt_kernel(q_ref, k_ref, v_ref, ccol_ref, crow_ref, o_ref, m_ref, l_ref, acc_ref,
                       *, tq, tk, nh, ff_lane, scale):
    i, j = pl.program_id(0), pl.program_id(1)

    @pl.when(j == 0)
    def _():
        m_ref[...] = jnp.full_like(m_ref, NEG_BIG)
        l_ref[...] = jnp.zeros_like(l_ref)
        acc_ref[...] = jnp.zeros_like(acc_ref)

    def block(diagonal):
        def sub(r, carry):
            rows = pl.ds(pl.multiple_of(r * sub_q, sub_q), sub_q)
            if diagonal:
                below = lax.broadcasted_iota(I32, (sub_q, tk), 1) <= r * sub_q + lax.broadcasted_iota(I32, (sub_q, tk), 0)
            for h in range(nh):
                hs = slice(h * HEAD_DIM, (h + 1) * HEAD_DIM)
                s = _dot_nt(q_ref[rows, hs], k_ref[:, hs]) * scale
                s = s + ccol_ref[rows, ff_lane + h:ff_lane + h + 1] - crow_ref[ff_lane + h:ff_lane + h + 1, :]
                if diagonal:
                    s = jnp.where(below, s, NEG_BIG)
                m = m_ref[h, rows, :]
                m_new = jnp.maximum(m, jnp.max(s, axis=1, keepdims=True))
                p = jnp.exp(s - m_new[:, 0:1])
                alpha = jnp.exp(m - m_new)
                l_ref[h, rows, :] = l_ref[h, rows, :] * alpha + jnp.sum(p, axis=1, keepdims=True)
                acc_ref[h, rows, :] = acc_ref[h, rows, :] * alpha + _dot(p.astype(BF16), v_ref[:, hs])
                m_ref[h, rows, :] = m_new
            return carry

        lax.fori_loop(0, tq // sub_q, sub, 0)

    sub_q = min(tq, 256)

    @pl.when(j < i)
    def _():
        block(False)

    @pl.when(j == i)
    def _():
        block(True)
        o_ref[...] = jnp.concatenate([acc_ref[h] / l_ref[h] for h in range(nh)], axis=1).astype(o_ref.dtype)


def _fox_prompt(fq_b, fk_b, fv_b, ccol, crow, tp, lay):
    bw = lay.bw
    tq = tk = _pick(tp, (512, 256, 128))
    kern = functools.partial(_fox_prompt_kernel, tq=tq, tk=tk, nh=lay.nh, ff_lane=lay.ff_lane,
                             scale=float(HEAD_DIM ** -0.5))
    return pl.pallas_call(
        kern,
        grid=(tp // tq, tp // tk),
        in_specs=[pl.BlockSpec((tq, bw), lambda i, j: (i, 0)),
                  pl.BlockSpec((tk, bw), lambda i, j: (jnp.minimum(j, i), 0)),
                  pl.BlockSpec((tk, bw), lambda i, j: (jnp.minimum(j, i), 0)),
                  pl.BlockSpec((tq, LANES), lambda i, j: (i, 0)),
                  pl.BlockSpec((LANES, tk), lambda i, j: (0, jnp.minimum(j, i)))],
        out_specs=pl.BlockSpec((tq, bw), lambda i, j: (i, 0)),
        out_shape=jax.ShapeDtypeStruct((tp, bw), BF16),
        scratch_shapes=[pltpu.VMEM((lay.nh, tq, LANES), F32), pltpu.VMEM((lay.nh, tq, LANES), F32),
                        pltpu.VMEM((lay.nh, tq, HEAD_DIM), F32)],
        compiler_params=_params(("parallel", "arbitrary")),
        name="fox_prompt",
    )(fq_b, fk_b, fv_b, ccol, crow)


def _fox_sample_kernel(pt_ref, q_ref, *rest, npg, g, page, ls, nh, scale):
    kp_refs, vp_refs, lf_refs = rest[:g], rest[g:2 * g], rest[2 * g:3 * g]
    kn_ref, vn_ref, lfn_ref, o_ref, carry_ref, q2_ref, m_ref, l_ref, acc_ref = rest[3 * g:]
    j = pl.program_id(1)

    @pl.when(j == 0)
    def _():
        _sample_attn_init(q_ref, q2_ref, m_ref, l_ref, acc_ref, nh)
        carry_ref[...] = jnp.zeros_like(carry_ref)

    own_head, new_valid, expand = _page_patterns(page, nh)
    upper = _upper_ones(page)

    def attend(kps, vps, lfs, valid):
        within = _dot(jnp.concatenate(lfs, axis=0), upper, HIGHEST)
        carry = carry_ref[:, 0:1]
        cs = []
        for n in range(len(lfs)):
            w = within[n * SAMPLE_ROWS:(n + 1) * SAMPLE_ROWS, :]
            cs.append(w + carry)
            carry = carry + w[:, page - 1:page]
        carry_ref[...] = jnp.broadcast_to(carry, carry_ref.shape)
        cs_x = _dot(jnp.concatenate(cs, axis=0), expand, HIGHEST)
        q2 = q2_ref[...]
        msk = own_head if valid is None else own_head & valid
        logits = []
        for n, kp in enumerate(kps):
            bias = jnp.concatenate([jnp.broadcast_to(cs_x[n * SAMPLE_ROWS + h:n * SAMPLE_ROWS + h + 1, :],
                                                     (SAMPLE_ROWS, page * nh)) for h in range(nh)], axis=0)
            logits.append(jnp.where(msk, _dot_nt(q2, kp.astype(BF16)) * scale - bias, NEG_BIG))
        _online_softmax_step(logits, [msk] * len(kps), [vp.astype(BF16) for vp in vps], m_ref, l_ref, acc_ref)

    attend([r[...] for r in kp_refs], [r[...] for r in vp_refs], [r[...] for r in lf_refs], None)

    @pl.when(j == npg // g - 1)
    def _():
        attend([kn_ref[...]], [vn_ref[...]], [lfn_ref[...]], new_valid(ls))
        o_ref[...] = _unstack_heads(acc_ref[...] / l_ref[...], nh).astype(o_ref.dtype)


def _fox_sample(page_table, q_s, kpool, vpool, lfpool_t, k_new, v_new, lf_new, layer, ls, lay):
    bs, npg = page_table.shape
    rows = kpool.shape[2]
    bw, nh = lay.bw, lay.nh
    page = rows // nh
    g = _pick(npg, ATTN_PAGES_PER_STEP)
    kern = functools.partial(_fox_sample_kernel, npg=npg, g=g, page=page, ls=ls, nh=nh,
                             scale=float(HEAD_DIM ** -0.5))
    per_seq = lambda shape: pl.BlockSpec((None,) + shape, lambda b, j, pt: (b, 0, 0))
    pages = [_page_spec((rows, HEAD_DIM), layer, g, gi) for gi in range(g)]
    grid_spec = pltpu.PrefetchScalarGridSpec(
        num_scalar_prefetch=1,
        grid=(bs, npg // g),
        in_specs=[per_seq((SAMPLE_ROWS, bw))] + pages + pages
        + [_page_spec((SAMPLE_ROWS, page), layer, g, gi) for gi in range(g)]
        + [per_seq((rows, HEAD_DIM)), per_seq((rows, HEAD_DIM)), per_seq((SAMPLE_ROWS, page))],
        out_specs=per_seq((SAMPLE_ROWS, bw)),
        scratch_shapes=[pltpu.VMEM((SAMPLE_ROWS, LANES), F32)] + _sample_attn_scratch(nh),
    )
    return pl.pallas_call(
        kern,
        grid_spec=grid_spec,
        out_shape=jax.ShapeDtypeStruct((bs, SAMPLE_ROWS, bw), BF16),
        compiler_params=_params(("parallel", "arbitrary")),
        name="fox_sample",
    )(page_table, q_s, *([kpool] * g), *([vpool] * g), *([lfpool_t] * g), k_new, v_new, lf_new)


def _hgrn_lower_bound(lg_ref, layer):
    lg = lg_ref[...]
    e = jnp.exp(lg - jnp.max(lg, axis=0, keepdims=True))
    p = e / jnp.sum(e, axis=0, keepdims=True)
    lb = jnp.zeros((1, lg.shape[1]), F32)
    for i in range(1, layer + 1):
        lb = lb + p[i:i + 1, :]
    return lb


def _hgrn_rows(hq, hf, hi, lb, chunk, row_valid, nh):
    r = hq.shape[0]
    pos = lax.broadcasted_iota(I32, hq.shape, 0) & (chunk - 1)
    lf = jnp.log(lb + (1.0 - lb) * jax.nn.sigmoid(hf))
    kk = (1.0 - lb) * jax.nn.sigmoid(-hf)
    if row_valid is not None:
        lf = jnp.where(row_valid, lf, 0.0)
        kk = jnp.where(row_valid, kk, 0.0)
    b = lf
    step = 1
    while step < chunk:
        b = b + jnp.where(pos >= step, pltpu.roll(b, step, 0), 0.0)
        step *= 2
    suf = lf
    step = 1
    while step < chunk:
        suf = suf + jnp.where(pos + step < chunk, pltpu.roll(suf, r - step, 0), 0.0)
        step *= 2
    tail = suf - lf
    qe = hq * jnp.exp(b)
    kd = kk * jnp.exp(tail)
    etot = jnp.exp(b + tail)
    o_intra = jnp.zeros_like(hq)
    for d in range(chunk):
        if d == 0:
            kk_d, b_d, i_d = kk, b, hi
        else:
            kk_d, b_d, i_d = pltpu.roll(kk, d, 0), pltpu.roll(b, d, 0), pltpu.roll(hi, d, 0)
        dec = jnp.exp(jnp.where(pos >= d, b - b_d, NEG_BIG))
        prod = hq * dec * kk_d
        a = jnp.concatenate(
            [jnp.broadcast_to(jnp.sum(prod[:, h * HEAD_DIM:(h + 1) * HEAD_DIM], axis=1, keepdims=True), (r, HEAD_DIM))
             for h in range(nh)], axis=1)
        o_intra = o_intra + a * i_d
    return qe, kd, etot, o_intra


def _hgrn_finish(o, hg, ng, nh):
    outs = []
    for h in range(nh):
        oh = o[:, h * HEAD_DIM:(h + 1) * HEAD_DIM]
        outs.append(oh * lax.rsqrt(jnp.mean(oh * oh, axis=1, keepdims=True) + EPS))
    return jnp.concatenate(outs, axis=1) * ng * (hg * jax.nn.sigmoid(hg))


def _hgrn_prompt_kernel(hq_ref, hf_ref, hi_ref, hg_ref, lg_ref, ng_ref, y_ref, s_ref,
                        st_ref, qe_ref, kd_ref, et_ref, oi_ref, *, layer, chunk, nh):
    @pl.when(pl.program_id(0) == 0)
    def _():
        st_ref[...] = jnp.zeros_like(st_ref)

    lb = _hgrn_lower_bound(lg_ref, layer)
    hi = hi_ref[...]
    qe, kd, etot, o_intra = _hgrn_rows(hq_ref[...], hf_ref[...], hi, lb, chunk, None, nh)
    qe_ref[...] = qe.astype(BF16)
    kd_ref[...] = kd.astype(BF16)
    et_ref[...] = etot
    tm = hi.shape[0]

    def chunk_body(c, carry):
        rows = pl.ds(pl.multiple_of(c * chunk, chunk), chunk)
        for h in range(nh):
            hs = slice(h * HEAD_DIM, (h + 1) * HEAD_DIM)
            st = st_ref[h]
            oi_ref[rows, hs] = _dot_nt(qe_ref[rows, hs], st.astype(BF16))
            e = et_ref[rows, hs]
            st_ref[h] = st * e[0:1, :] + _dot_tn(hi_ref[rows, hs].astype(BF16), kd_ref[rows, hs])
        return carry

    lax.fori_loop(0, tm // chunk, chunk_body, 0, unroll=2)
    y_ref[...] = _hgrn_finish(oi_ref[...] + o_intra, hg_ref[...], ng_ref[...], nh).astype(y_ref.dtype)

    @pl.when(pl.program_id(0) == pl.num_programs(0) - 1)
    def _():
        for h in range(nh):
            s_ref[h] = st_ref[h].T


def _hgrn_prompt(z, lb_logits, norm_g, tp, lay, layer):
    bw, nh = lay.bw, lay.nh
    tm = _pick(tp, (512, 256, 128))
    dep = lb_logits.shape[0]
    col = lambda off: pl.BlockSpec((tm, bw), lambda i: (i, off // bw))
    kern = functools.partial(_hgrn_prompt_kernel, layer=layer, chunk=HG_CHUNK_PROMPT, nh=nh)
    return pl.pallas_call(
        kern,
        grid=(tp // tm,),
        in_specs=[col(lay.hq), col(lay.hf), col(lay.hi), col(lay.hg),
                  pl.BlockSpec((dep, bw), lambda i: (0, 0)),
                  pl.BlockSpec((None, 1, bw), lambda i: (layer, 0, 0))],
        out_specs=[pl.BlockSpec((tm, bw), lambda i: (i, 0)),
                   pl.BlockSpec((nh, HEAD_DIM, HEAD_DIM), lambda i: (0, 0, 0))],
        out_shape=[jax.ShapeDtypeStruct((tp, bw), BF16), jax.ShapeDtypeStruct((nh, HEAD_DIM, HEAD_DIM), F32)],
        scratch_shapes=[pltpu.VMEM((nh, HEAD_DIM, HEAD_DIM), F32), pltpu.VMEM((tm, bw), BF16),
                        pltpu.VMEM((tm, bw), BF16), pltpu.VMEM((tm, bw), F32), pltpu.VMEM((tm, bw), F32)],
        compiler_params=_params(("arbitrary",)),
        name="hgrn_prompt",
    )(z, z, z, z, lb_logits, norm_g)


def _hgrn_sample_kernel(hq_ref, hf_ref, hi_ref, hg_ref, lg_ref, ng_ref, s0_ref, y_ref, s_ref,
                        *, layer, ls, nh):
    lb = _hgrn_lower_bound(lg_ref, layer)
    hi = hi_ref[...]
    row_valid = lax.broadcasted_iota(I32, hi.shape, 0) < ls
    hi = jnp.where(row_valid, hi, 0.0)
    qe, kd, etot, o_intra = _hgrn_rows(hq_ref[...], hf_ref[...], hi, lb, SAMPLE_ROWS, row_valid, nh)
    outs = []
    for h in range(nh):
        hs = slice(h * HEAD_DIM, (h + 1) * HEAD_DIM)
        st = s0_ref[h].T
        outs.append(_dot_nt(qe[:, hs].astype(BF16), st.astype(BF16)))
        st = st * etot[0:1, hs] + _dot_tn(hi[:, hs].astype(BF16), kd[:, hs].astype(BF16))
        s_ref[h] = st.T
    o = jnp.concatenate(outs, axis=1) + o_intra
    y_ref[...] = _hgrn_finish(o, hg_ref[...], ng_ref[...], nh).astype(y_ref.dtype)


def _hgrn_sample(hq, hf, hi, hg, lb_logits, norm_g, state, layer, ls, lay):
    bs = hq.shape[0]
    bw, nh = lay.bw, lay.nh
    dep = lb_logits.shape[0]
    rows = pl.BlockSpec((None, SAMPLE_ROWS, bw), lambda b: (b, 0, 0))
    kern = functools.partial(_hgrn_sample_kernel, layer=layer, ls=ls, nh=nh)
    return pl.pallas_call(
        kern,
        grid=(bs,),
        in_specs=[rows, rows, rows, rows,
                  pl.BlockSpec((dep, bw), lambda b: (0, 0)),
                  pl.BlockSpec((None, 1, bw), lambda b: (layer, 0, 0)),
                  pl.BlockSpec((None, None, nh, HEAD_DIM, HEAD_DIM), lambda b: (layer, b, 0, 0, 0))],
        out_specs=[rows, pl.BlockSpec((None, nh, HEAD_DIM, HEAD_DIM), lambda b: (b, 0, 0, 0))],
        out_shape=[jax.ShapeDtypeStruct((bs, SAMPLE_ROWS, bw), BF16),
                   jax.ShapeDtypeStruct((bs, nh, HEAD_DIM, HEAD_DIM), F32)],
        compiler_params=_params(("parallel",)),
        name="hgrn_sample",
    )(hq, hf, hi, hg, lb_logits, norm_g, state)


def _pad_rows(a, bs, ls, rows):
    return jnp.pad(a.reshape(bs, ls, a.shape[-1]), ((0, 0), (0, rows - ls), (0, 0)))


def kernel(x_prompt, x_sample, cache_dsa_k, cache_dsa_v, cache_dsa_kidx, cache_fox_k, cache_fox_v,
           cache_fox_logf, state_conv, state_hgrn, page_table, norm_mix_g, w_in, conv_w, fox_fb,
           hgrn_lb_logits, hgrn_norm_g, w_branch, w_out, norm_ffn_g, w_up, w_down, final_norm_g):
    bp, tp, d = x_prompt.shape
    bs, ls, _ = x_sample.shape
    assert bp == 1 and ls <= SAMPLE_ROWS
    depth = w_in.shape[0]
    lay = _Layout(d)
    bw, nh = lay.bw, lay.nh
    ts = bs * ls
    npg = page_table.shape[1]
    n_pool, page = cache_dsa_k.shape[1], cache_dsa_k.shape[2]
    assert page == LANES and tp % LANES == 0
    past = npg * page

    pos = jnp.concatenate([jnp.arange(tp), past + jnp.tile(jnp.arange(ls), bs)])
    tab_a, half_a = _rope_tables(pos, HEAD_DIM)
    tab_i, half_i = _rope_tables(pos, IDX_DIM)

    w_in_r = _relayout_w_in(w_in, lay)
    w_br_b, w_out_b = w_branch.astype(BF16), w_out.astype(BF16)
    w_up_b, w_down_b = w_up.astype(BF16), w_down.astype(BF16)
    fb_rows = jnp.pad(fox_fb, ((0, 0), (lay.ff_lane, LANES - lay.ff_lane - nh)))[:, None, :]
    g_mix, g_ffn, g_hg = norm_mix_g[:, None, :], norm_ffn_g[:, None, :], hgrn_norm_g[:, None, :]
    as_pages = lambda c: c.reshape(depth, n_pool, page * nh, HEAD_DIM)
    kpool_a, vpool_a, kpool_f, vpool_f = map(as_pages, (cache_dsa_k, cache_dsa_v, cache_fox_k, cache_fox_v))
    new_page = lambda a: jnp.pad(a.reshape(bs, ls * nh, HEAD_DIM), ((0, 0), (0, (page - ls) * nh), (0, 0)))
    ikpool_t = jnp.swapaxes(cache_dsa_kidx, 2, 3)
    lfpool_t = jnp.pad(jnp.swapaxes(cache_fox_logf, 2, 3), ((0, 0), (0, 0), (0, SAMPLE_ROWS - nh), (0, 0)))
    conv_state = state_conv.reshape(depth, bs, (CONV_W - 1) * bw)

    x = jnp.concatenate([x_prompt[0], x_sample.reshape(ts, d)], axis=0)
    new_p, new_s = [], []
    for l in range(depth):
        z = _in_proj(x, g_mix, w_in_r[l], l)
        (aq_r, ak_r, ak_b, av_b, iq_r, ik_r, ik_b, misc2,
         fq_b, fk_b, fv_b) = _post(z, tab_a, tab_i, fb_rows, lay, l, half_a, half_i)
        zs = z[tp:]
        sec = lambda off: zs[:, off:off + bw]

        y_conv_p, u_last = _conv_prompt(z, conv_w, tp, lay, l)
        flat = lambda a: a.reshape(bs, ls * bw)
        y_conv_s, conv_new_s = _conv_sample(flat(sec(lay.ch)), flat(sec(lay.cb)), flat(sec(lay.cc)),
                                            conv_state, conv_w, l, bs, ls, bw)

        y_dsa_p = _dsa_prompt(iq_r, misc2, aq_r, ik_b, ak_b, av_b, tp, lay)
        iq_s = _pad_rows(iq_r[tp:], bs, ls, SAMPLE_ROWS).reshape(bs, SAMPLE_ROWS, IDX_HEADS, LANES)
        iq_s = jnp.swapaxes(iq_s, 1, 2).reshape(bs, IDX_HEADS * SAMPLE_ROWS, LANES)
        iw_s = jnp.swapaxes(_pad_rows(misc2[tp:, :IDX_HEADS], bs, ls, SAMPLE_ROWS), 1, 2)
        iw_s = jnp.broadcast_to(iw_s.reshape(bs, IDX_HEADS * SAMPLE_ROWS, 1), (bs, IDX_HEADS * SAMPLE_ROWS, LANES))
        ik_new = jnp.swapaxes(_pad_rows(ik_r[tp:, :IDX_DIM], bs, ls, page), 1, 2)
        keys, keys_new = _dsa_sample_score(page_table, iq_s, iw_s, ikpool_t, ik_new, l, ls)
        tau = _dsa_sample_thresh(keys, keys_new, ls)
        y_dsa_s = _dsa_sample_attn(page_table, _pad_rows(aq_r[tp:], bs, ls, SAMPLE_ROWS), keys, keys_new, tau,
                                   kpool_a, vpool_a, new_page(ak_b[tp:]), new_page(av_b[tp:]), l, ls, lay)

        crow, ccol = _fox_cumsum(misc2, tp)
        y_fox_p = _fox_prompt(fq_b, fk_b, fv_b, ccol, crow, tp, lay)
        logf_s = misc2[tp:, lay.ff_lane:lay.ff_lane + nh]
        lf_new = jnp.pad(jnp.swapaxes(logf_s.reshape(bs, ls, nh), 1, 2),
                         ((0, 0), (0, SAMPLE_ROWS - nh), (0, page - ls)))
        y_fox_s = _fox_sample(page_table, _pad_rows(fq_b[tp:], bs, ls, SAMPLE_ROWS), kpool_f, vpool_f, lfpool_t,
                              new_page(fk_b[tp:]), new_page(fv_b[tp:]), lf_new, l, ls, lay)

        y_hg_p, s_hg_p = _hgrn_prompt(z, hgrn_lb_logits, g_hg, tp, lay, l)
        pad8 = lambda off: _pad_rows(sec(off), bs, ls, SAMPLE_ROWS)
        y_hg_s, s_hg_s = _hgrn_sample(pad8(lay.hq), pad8(lay.hf), pad8(lay.hi), pad8(lay.hg),
                                      hgrn_lb_logits, g_hg, state_hgrn, l, ls, lay)

        unpad = lambda a: a[:, :ls].reshape(ts, bw)
        branches = [jnp.concatenate([y_conv_p, y_conv_s.reshape(ts, bw)], axis=0),
                    jnp.concatenate([y_dsa_p, unpad(y_dsa_s)], axis=0),
                    jnp.concatenate([y_fox_p, unpad(y_fox_s)], axis=0),
                    jnp.concatenate([y_hg_p, unpad(y_hg_s)], axis=0)]
        x = _mix(branches, z, w_br_b, w_out_b, x, lay, l)
        x = _ffn(x, g_ffn, w_up_b, w_down_b, l)

        heads = lambda a, b_, l_: a.reshape(b_, l_, nh, HEAD_DIM)
        zp = lambda off: z[:tp, off:off + bw]
        new_p.append((heads(ak_r[:tp], 1, tp), heads(zp(lay.av), 1, tp), ik_r[:tp, :IDX_DIM].reshape(1, tp, IDX_DIM),
                      heads(zp(lay.fk), 1, tp), heads(zp(lay.fv), 1, tp),
                      misc2[:tp, lay.ff_lane:lay.ff_lane + nh].reshape(1, tp, nh),
                      u_last[8 - (CONV_W - 1):].reshape(1, CONV_W - 1, bw), s_hg_p[None]))
        new_s.append((heads(ak_r[tp:], bs, ls), heads(sec(lay.av), bs, ls), ik_r[tp:, :IDX_DIM].reshape(bs, ls, IDX_DIM),
                      heads(sec(lay.fk), bs, ls), heads(sec(lay.fv), bs, ls), logf_s.reshape(bs, ls, nh),
                      conv_new_s.reshape(bs, CONV_W - 1, bw), s_hg_s))

    y = _final_norm(x, final_norm_g[None, :])
    y_prompt = y[:tp][None]
    y_sample = y[tp:].reshape(bs, ls, d)
    outs_p = [jnp.stack(t) for t in zip(*new_p)]
    outs_s = [jnp.stack(t) for t in zip(*new_s)]
    return (y_prompt, y_sample, *outs_p, *outs_s)
```
